```python
import math
import jax, jax.numpy as jnp
from jax import lax
import numpy as np

D_MODEL = 1024
BATCH = 32
SEQ = 256
DEPTH = 2
DEC_BATCH = 2
DEC_SEQ = 2048
PAST_LEN = 512

GRID_W = 64
CHUNK = 128
SHORT_CONV_W = 5
FFN_CONV_W = 3
N_MOD = 6
EPS = 1e-6

SSD_HEADS = 16
SSD_HEAD_DIM = 64
SSD_INNER = SSD_HEADS * SSD_HEAD_DIM
SSD_GROUPS = 4
SSD_STATE = 128
DN_HEADS = 8
DN_KEY_DIM = 128
DN_VAL_DIM = 128
DN_QK = DN_HEADS * DN_KEY_DIM
DN_V = DN_HEADS * DN_VAL_DIM
SG_GROUPS = 4
SG_WIDTH = 1024
SG_GROUP_DIM = SG_WIDTH // SG_GROUPS
ATTN_HEADS = 8
ATTN_KV_HEADS = 2
ATTN_GROUP = ATTN_HEADS // ATTN_KV_HEADS
ATTN_HEAD_DIM = 128
WINDOW = 128
ROPE_THETA = 10000.0
ROPE_PAIRS_AXIS = ATTN_HEAD_DIM // 4
D_FF = 2816

L0_SIZES = (SSD_INNER, SSD_INNER + 2 * SSD_GROUPS * SSD_STATE, 2 * SSD_HEADS,
            2 * DN_QK + DN_V, DN_V, 2 * DN_HEADS, 2 * DN_HEADS)
L0_PROJ = sum(L0_SIZES)
L0_MIX = SSD_INNER + DN_V
L1_SIZES = (2 * SG_WIDTH, ATTN_HEADS * ATTN_HEAD_DIM, ATTN_KV_HEADS * ATTN_HEAD_DIM, ATTN_KV_HEADS * ATTN_HEAD_DIM)
L1_PROJ = sum(L1_SIZES)
L1_MIX = SG_WIDTH + ATTN_HEADS * ATTN_HEAD_DIM

F32 = jnp.float32

kernel_name = 'hybrid_diffusion_prefix_step'


def _offsets(sizes):
    out, acc = [], 0
    for s in sizes[:-1]:
        acc += s
        out.append(acc)
    return out


def flip(t):
    return t[:, ::-1]


def rmsnorm(x, w):
    xf = x.astype(F32)
    y = xf * lax.rsqrt(jnp.mean(xf * xf, axis=-1, keepdims=True) + EPS)
    return (y * w.astype(F32)).astype(x.dtype)


def layernorm(x, w, b):
    xf = x.astype(F32)
    xc = xf - jnp.mean(xf, axis=-1, keepdims=True)
    y = xc * lax.rsqrt(jnp.mean(xc * xc, axis=-1, keepdims=True) + EPS)
    return (y * w.astype(F32) + b.astype(F32)).astype(x.dtype)


def l2norm(x):
    xf = x.astype(F32)
    return xf * lax.rsqrt(jnp.sum(xf * xf, axis=-1, keepdims=True) + EPS)


def dwconv(x, w, b=None):
    width = w.shape[0]
    pad = width // 2
    length = x.shape[1]
    xp = jnp.pad(x, ((0, 0), (pad, pad), (0, 0)))
    y = xp[:, 0:length] * w[0]
    for tap in range(1, width):
        y = y + xp[:, tap:tap + length] * w[tap]
    return y if b is None else y + b


def modulated_rmsnorm(x, w, shift, scale):
    return rmsnorm(x, w) * (1.0 + scale[:, None, :]) + shift[:, None, :]


def adaln(cond, w, b):
    return (jax.nn.silu(cond.astype(F32)) @ w + b).reshape(cond.shape[0], N_MOD, D_MODEL)


def ssd_scan(x, dt, A, Bm, Cm, h0):
    bsz, length, nh, hp = x.shape
    ns = Bm.shape[-1]
    nc = length // CHUNK
    xc = x.astype(F32).reshape(bsz, nc, CHUNK, nh, hp)
    bc = Bm.astype(F32).reshape(bsz, nc, CHUNK, nh, ns)
    cc = Cm.astype(F32).reshape(bsz, nc, CHUNK, nh, ns)
    dtc = dt.astype(F32).reshape(bsz, nc, CHUNK, nh)
    acs = jnp.cumsum(dtc * A.astype(F32), axis=2)
    tril = jnp.tril(jnp.ones((CHUNK, CHUNK), bool))[:, :, None]
    seg = acs[:, :, :, None, :] - acs[:, :, None, :, :]
    lmat = jnp.exp(jnp.where(tril, seg, -jnp.inf))
    xdt = xc * dtc[..., None]
    scores = jnp.einsum('bcihn,bcjhn->bcijh', cc, bc) * lmat
    y_diag = jnp.einsum('bcijh,bcjhp->bcihp', scores, xdt)
    decay_to_end = jnp.exp(acs[:, :, -1:, :] - acs)
    chunk_states = jnp.einsum('bcjhn,bcjh,bcjhp->bchpn', bc, decay_to_end, xdt)
    chunk_decay = jnp.exp(acs[:, :, -1, :])

    def step(h, inp):
        st, dec = inp
        return h * dec[:, :, None, None] + st, h

    h_fin, h_prev = lax.scan(step, h0.astype(F32),
                             (jnp.moveaxis(chunk_states, 1, 0), jnp.moveaxis(chunk_decay, 1, 0)))
    h_prev = jnp.moveaxis(h_prev, 0, 1)
    y_off = jnp.einsum('bcihn,bchpn,bcih->bcihp', cc, h_prev, jnp.exp(acs))
    return (y_diag + y_off).reshape(bsz, length, nh, hp), h_fin


def gated_delta_scan(q, k, v, g, beta, s0):
    bsz, length, nh, dk = q.shape
    dv = v.shape[-1]
    nc = length // CHUNK

    def to_chunks(t):
        t = t.astype(F32).reshape((bsz, nc, CHUNK) + t.shape[2:])
        return jnp.moveaxis(t, 3, 2)

    qc = to_chunks(q) * (dk ** -0.5)
    kc = to_chunks(k)
    vc = to_chunks(v)
    bc = to_chunks(beta)
    gcs = jnp.cumsum(to_chunks(g), axis=-1)
    tril = jnp.tril(jnp.ones((CHUNK, CHUNK), bool))
    strict = jnp.tril(jnp.ones((CHUNK, CHUNK), bool), -1)
    decay = jnp.exp(jnp.where(tril, gcs[..., :, None] - gcs[..., None, :], -jnp.inf))
    kbeta = kc * bc[..., None]
    a_strict = jnp.where(strict, jnp.einsum('bchid,bchjd->bchij', kbeta, kc) * decay, 0.0)
    ia = a_strict + jnp.eye(CHUNK, dtype=F32)
    u = lax.linalg.triangular_solve(ia, vc * bc[..., None], left_side=True, lower=True)
    w = lax.linalg.triangular_solve(ia, kbeta * jnp.exp(gcs)[..., None], left_side=True, lower=True)
    qk = jnp.einsum('bchid,bchjd->bchij', qc, kc) * decay
    q_dec = qc * jnp.exp(gcs)[..., None]
    k_end = kc * jnp.exp(gcs[..., -1:] - gcs)[..., None]
    last = jnp.exp(gcs[..., -1])

    def step(s, inp):
        qk_c, qd_c, w_c, u_c, ke_c, last_c = inp
        v_new = u_c - jnp.einsum('bhik,bhkv->bhiv', w_c, s)
        o_c = jnp.einsum('bhik,bhkv->bhiv', qd_c, s) + jnp.einsum('bhij,bhjv->bhiv', qk_c, v_new)
        s = s * last_c[..., None, None] + jnp.einsum('bhjk,bhjv->bhkv', ke_c, v_new)
        return s, o_c

    xs = (jnp.moveaxis(qk, 1, 0), jnp.moveaxis(q_dec, 1, 0), jnp.moveaxis(w, 1, 0),
          jnp.moveaxis(u, 1, 0), jnp.moveaxis(k_end, 1, 0), jnp.moveaxis(last, 1, 0))
    s_fin, o = lax.scan(step, s0.astype(F32), xs)
    o = jnp.moveaxis(jnp.moveaxis(o, 0, 1), 2, 3).reshape(bsz, length, nh, dv)
    return o, s_fin


def ssd_delta_mixer(h, ssd_h0, dn_h0, w_in, w_out, ssd_conv_w, ssd_conv_b, ssd_dt_bias, ssd_A_log,
                    ssd_D, ssd_norm_w, dn_conv_w, dn_dt_bias, dn_A_log, dn_norm_w):
    bsz, length, _ = h.shape
    z, xbc, dt_raw, qkv, gate, a_raw, b_raw = jnp.split(h @ w_in, _offsets(L0_SIZES), axis=-1)
    xbc = jax.nn.silu(dwconv(xbc, ssd_conv_w, ssd_conv_b))
    xs, bm, cm = jnp.split(xbc, [SSD_INNER, SSD_INNER + SSD_GROUPS * SSD_STATE], axis=-1)
    xs = xs.reshape(bsz, length, SSD_HEADS, SSD_HEAD_DIM)
    rep = SSD_HEADS // SSD_GROUPS
    bh = jnp.repeat(bm.reshape(bsz, length, SSD_GROUPS, SSD_STATE), rep, axis=2)
    ch = jnp.repeat(cm.reshape(bsz, length, SSD_GROUPS, SSD_STATE), rep, axis=2)
    dt = jax.nn.softplus(dt_raw.astype(F32).reshape(bsz, length, 2, SSD_HEADS) + ssd_dt_bias.astype(F32))
    a_ssd = -jnp.exp(ssd_A_log.astype(F32))
    y_f, h_f = ssd_scan(xs, dt[:, :, 0], a_ssd[0], bh, ch, ssd_h0[:, 0])
    y_b, h_b = ssd_scan(flip(xs), flip(dt[:, :, 1]), a_ssd[1], flip(bh), flip(ch), ssd_h0[:, 1])
    y = y_f + flip(y_b) + ssd_D.astype(F32)[:, None] * xs.astype(F32)
    y = rmsnorm(y.reshape(bsz, length, SSD_INNER) * jax.nn.silu(z.astype(F32)), ssd_norm_w)
    qkv = jax.nn.silu(dwconv(qkv, dn_conv_w))
    q, k, v = jnp.split(qkv, [DN_QK, 2 * DN_QK], axis=-1)
    q = l2norm(q.reshape(bsz, length, DN_HEADS, DN_KEY_DIM))
    k = l2norm(k.reshape(bsz, length, DN_HEADS, DN_KEY_DIM))
    v = v.reshape(bsz, length, DN_HEADS, DN_VAL_DIM)
    g = -jnp.exp(dn_A_log.astype(F32)) * jax.nn.softplus(
        a_raw.astype(F32).reshape(bsz, length, 2, DN_HEADS) + dn_dt_bias.astype(F32))
    beta = jax.nn.sigmoid(b_raw.astype(F32).reshape(bsz, length, 2, DN_HEADS))
    o_f, s_f = gated_delta_scan(q, k, v, g[:, :, 0], beta[:, :, 0], dn_h0[:, 0])
    o_b, s_b = gated_delta_scan(flip(q), flip(k), flip(v), flip(g[:, :, 1]), flip(beta[:, :, 1]), dn_h0[:, 1])
    o = rmsnorm(o_f + flip(o_b), dn_norm_w) * jax.nn.silu(gate.astype(F32)).reshape(bsz, length, DN_HEADS, DN_VAL_DIM)
    mixed = jnp.concatenate([y, o.reshape(bsz, length, DN_V)], axis=-1).astype(h.dtype)
    return mixed @ w_out, jnp.stack([h_f, h_b], axis=1), jnp.stack([s_f, s_b], axis=1)


def axial_rope_tables(length):
    rows = length // GRID_W
    row = jnp.repeat(jnp.arange(rows, dtype=F32), GRID_W)
    col = jnp.tile(jnp.arange(GRID_W, dtype=F32), rows)
    inv = ROPE_THETA ** (-jnp.arange(ROPE_PAIRS_AXIS, dtype=F32) / ROPE_PAIRS_AXIS)
    ang = jnp.concatenate([row[:, None] * inv, col[:, None] * inv], axis=-1)
    return jnp.cos(ang), jnp.sin(ang)


def apply_rope(x, cos, sin):
    half = x.shape[-1] // 2
    x1 = x[..., :half].astype(F32)
    x2 = x[..., half:].astype(F32)
    c = cos[None, :, None, :]
    s = sin[None, :, None, :]
    return jnp.concatenate([x1 * c - x2 * s, x2 * c + x1 * s], axis=-1).astype(x.dtype)


def sink_attend(q, k, v, mask, sink):
    s = jnp.einsum('bqkgd,bskd->bkgqs', q.astype(F32), k.astype(F32)) * (ATTN_HEAD_DIM ** -0.5)
    if mask is not None:
        s = jnp.where(mask, s, -jnp.inf)
    sink_col = jnp.broadcast_to(sink.astype(F32).reshape(ATTN_KV_HEADS, ATTN_GROUP)[None, :, :, None, None],
                                s.shape[:-1] + (1,))
    p = jax.nn.softmax(jnp.concatenate([s, sink_col], axis=-1), axis=-1)[..., :-1]
    o = jnp.einsum('bkgqs,bskd->bqkgd', p, v.astype(F32))
    return o.reshape(o.shape[0], o.shape[1], ATTN_HEADS * ATTN_HEAD_DIM)


def context_attention(q, k, v, sink):
    bsz, length = q.shape[:2]
    nb = length // CHUNK
    qb = jnp.swapaxes(q.reshape(bsz, nb, CHUNK, ATTN_KV_HEADS, ATTN_GROUP, ATTN_HEAD_DIM), 0, 1)
    ob = lax.map(lambda qq: sink_attend(qq, k, v, None, sink), qb)
    return jnp.swapaxes(ob, 0, 1).reshape(bsz, length, ATTN_HEADS * ATTN_HEAD_DIM)


def latent_attention(q, k, v, k_ctx, v_ctx, sink):
    bsz, length = q.shape[:2]
    nb = length // CHUNK
    band = 3 * CHUNK
    pad = ((0, 0), (CHUNK, CHUNK), (0, 0), (0, 0))
    kp = jnp.pad(k, pad)
    vp = jnp.pad(v, pad)
    k_ctx = k_ctx.astype(k.dtype)
    v_ctx = v_ctx.astype(v.dtype)
    n_ctx = k_ctx.shape[1]
    qb = jnp.swapaxes(q.reshape(bsz, nb, CHUNK, ATTN_KV_HEADS, ATTN_GROUP, ATTN_HEAD_DIM), 0, 1)

    def block(args):
        i, qq = args
        start = i * CHUNK
        kb = jnp.concatenate([lax.dynamic_slice_in_dim(kp, start, band, axis=1), k_ctx], axis=1)
        vb = jnp.concatenate([lax.dynamic_slice_in_dim(vp, start, band, axis=1), v_ctx], axis=1)
        qpos = start + jnp.arange(CHUNK)
        kpos = start - CHUNK + jnp.arange(band)
        near = ((jnp.abs(qpos[:, None] - kpos[None, :]) <= WINDOW)
                & (kpos >= 0)[None, :] & (kpos < length)[None, :])
        mask = jnp.concatenate([near, jnp.ones((CHUNK, n_ctx), bool)], axis=1)
        return sink_attend(qq, kb, vb, mask, sink)

    ob = lax.map(block, (jnp.arange(nb), qb))
    return jnp.swapaxes(ob, 0, 1).reshape(bsz, length, ATTN_HEADS * ATTN_HEAD_DIM)


def _l1_branches(h, w_in, sg_ln_w, sg_ln_b, sg_w_s, sg_b_s):
    bsz, length, _ = h.shape
    uv, q, k, v = jnp.split(h @ w_in, _offsets(L1_SIZES), axis=-1)
    u, gv = jnp.split(jax.nn.gelu(uv), 2, axis=-1)
    gv = layernorm(gv, sg_ln_w, sg_ln_b).reshape(bsz, length // CHUNK, CHUNK, SG_GROUPS, SG_GROUP_DIM)
    sv = jnp.einsum('gij,bcjgd->bcigd', sg_w_s, gv) + sg_b_s.T[:, :, None]
    mlp_out = u * sv.reshape(bsz, length, SG_WIDTH)
    q = q.reshape(bsz, length, ATTN_HEADS, ATTN_HEAD_DIM)
    k = k.reshape(bsz, length, ATTN_KV_HEADS, ATTN_HEAD_DIM)
    v = v.reshape(bsz, length, ATTN_KV_HEADS, ATTN_HEAD_DIM)
    return mlp_out, q, k, v


def l1_mixer_context(h, w_in, w_out, sg_ln_w, sg_ln_b, sg_w_s, sg_b_s, attn_sink):
    mlp_out, q, k, v = _l1_branches(h, w_in, sg_ln_w, sg_ln_b, sg_w_s, sg_b_s)
    attn_out = context_attention(q, k, v, attn_sink)
    out = jnp.concatenate([mlp_out, attn_out.astype(mlp_out.dtype)], axis=-1) @ w_out
    return out, k, v


def l1_mixer_latent(h, k_ctx, v_ctx, w_in, w_out, sg_ln_w, sg_ln_b, sg_w_s, sg_b_s, attn_sink):
    mlp_out, q, k, v = _l1_branches(h, w_in, sg_ln_w, sg_ln_b, sg_w_s, sg_b_s)
    cos, sin = axial_rope_tables(h.shape[1])
    attn_out = latent_attention(apply_rope(q, cos, sin), apply_rope(k, cos, sin), v, k_ctx, v_ctx, attn_sink)
    return jnp.concatenate([mlp_out, attn_out.astype(mlp_out.dtype)], axis=-1) @ w_out


def conv_ffn(h, w_up, conv_w, conv_b, w_down):
    a, b = jnp.split(dwconv(h @ w_up, conv_w, conv_b), 2, axis=-1)
    return (jax.nn.silu(a) * b) @ w_down


def setup_inputs(seed: int = 0) -> dict:
    key = jax.random.key(seed)
    keys = iter(jax.random.split(key, 64))

    def nrm(shape, scale=1.0):
        return scale * jax.random.normal(next(keys), shape, F32)

    def gain(n):
        return 1.0 + 0.1 * jax.random.normal(next(keys), (n,), F32)

    def dt_bias(shape):
        dt = jnp.exp(jax.random.uniform(next(keys), shape, F32, math.log(1e-3), math.log(1e-1)))
        return dt + jnp.log(-jnp.expm1(-dt))

    def a_log(shape):
        return jnp.log(jax.random.uniform(next(keys), shape, F32, 1.0, 16.0))

    inp = {}
    inp['x_prompt'] = nrm((BATCH, SEQ, D_MODEL))
    inp['x_sample'] = nrm((DEC_BATCH, DEC_SEQ, D_MODEL))
    inp['state_l0_ssd'] = nrm((DEC_BATCH, 2, SSD_HEADS, SSD_HEAD_DIM, SSD_STATE), 0.5)
    inp['state_l0_dn'] = nrm((DEC_BATCH, 2, DN_HEADS, DN_KEY_DIM, DN_VAL_DIM), 0.5)
    inp['cache_l1_k'] = nrm((DEC_BATCH, PAST_LEN, ATTN_KV_HEADS, ATTN_HEAD_DIM))
    inp['cache_l1_v'] = nrm((DEC_BATCH, PAST_LEN, ATTN_KV_HEADS, ATTN_HEAD_DIM))
    inp['c'] = nrm((DEC_BATCH, D_MODEL))
    inp['c_ctx'] = nrm((D_MODEL,))
    for l in range(DEPTH):
        inp['mod_w_l%d' % l] = nrm((D_MODEL, N_MOD * D_MODEL), 0.5 * D_MODEL ** -0.5)
        inp['mod_b_l%d' % l] = nrm((N_MOD * D_MODEL,), 0.02)
        inp['norm_mix_pre_l%d' % l] = gain(D_MODEL)
        inp['norm_mix_post_l%d' % l] = gain(D_MODEL)
        inp['norm_ffn_pre_l%d' % l] = gain(D_MODEL)
        inp['norm_ffn_post_l%d' % l] = gain(D_MODEL)
        inp['ffn_up_l%d' % l] = nrm((D_MODEL, 2 * D_FF), D_MODEL ** -0.5)
        inp['ffn_conv_w_l%d' % l] = nrm((FFN_CONV_W, 2 * D_FF), FFN_CONV_W ** -0.5)
        inp['ffn_conv_b_l%d' % l] = nrm((2 * D_FF,), 0.02)
        inp['ffn_down_l%d' % l] = nrm((D_FF, D_MODEL), D_FF ** -0.5)
    inp['mix_in_l0'] = nrm((D_MODEL, L0_PROJ), D_MODEL ** -0.5)
    inp['mix_out_l0'] = nrm((L0_MIX, D_MODEL), L0_MIX ** -0.5)
    inp['ssd_conv_w'] = nrm((SHORT_CONV_W, SSD_INNER + 2 * SSD_GROUPS * SSD_STATE), SHORT_CONV_W ** -0.5)
    inp['ssd_conv_b'] = nrm((SSD_INNER + 2 * SSD_GROUPS * SSD_STATE,), 0.02)
    inp['ssd_dt_bias'] = dt_bias((2, SSD_HEADS))
    inp['ssd_A_log'] = a_log((2, SSD_HEADS))
    inp['ssd_D'] = gain(SSD_HEADS)
    inp['ssd_norm_w'] = gain(SSD_INNER)
    inp['dn_conv_w'] = nrm((SHORT_CONV_W, 2 * DN_QK + DN_V), SHORT_CONV_W ** -0.5)
    inp['dn_dt_bias'] = dt_bias((2, DN_HEADS))
    inp['dn_A_log'] = a_log((2, DN_HEADS))
    inp['dn_norm_w'] = gain(DN_VAL_DIM)
    inp['mix_in_l1'] = nrm((D_MODEL, L1_PROJ), D_MODEL ** -0.5)
    inp['mix_out_l1'] = nrm((L1_MIX, D_MODEL), L1_MIX ** -0.5)
    inp['sg_ln_w'] = gain(SG_WIDTH)
    inp['sg_ln_b'] = nrm((SG_WIDTH,), 0.02)
    inp['sg_w_s'] = nrm((SG_GROUPS, CHUNK, CHUNK), CHUNK ** -0.5)
    inp['sg_b_s'] = 1.0 + nrm((SG_GROUPS, CHUNK), 0.1)
    inp['attn_sink'] = nrm((ATTN_HEADS,))
    return inp


def reference(x_prompt, x_sample, state_l0_ssd, state_l0_dn, cache_l1_k, cache_l1_v, c, c_ctx,
              mod_w_l0, mod_b_l0, norm_mix_pre_l0, norm_mix_post_l0, norm_ffn_pre_l0, norm_ffn_post_l0,
              ffn_up_l0, ffn_conv_w_l0, ffn_conv_b_l0, ffn_down_l0,
              mod_w_l1, mod_b_l1, norm_mix_pre_l1, norm_mix_post_l1, norm_ffn_pre_l1, norm_ffn_post_l1,
              ffn_up_l1, ffn_conv_w_l1, ffn_conv_b_l1, ffn_down_l1,
              mix_in_l0, mix_out_l0, ssd_conv_w, ssd_conv_b, ssd_dt_bias, ssd_A_log, ssd_D, ssd_norm_w,
              dn_conv_w, dn_dt_bias, dn_A_log, dn_norm_w,
              mix_in_l1, mix_out_l1, sg_ln_w, sg_ln_b, sg_w_s, sg_b_s, attn_sink):
    mod_w = (mod_w_l0, mod_w_l1)
    mod_b = (mod_b_l0, mod_b_l1)
    n_mix_pre = (norm_mix_pre_l0, norm_mix_pre_l1)
    n_mix_post = (norm_mix_post_l0, norm_mix_post_l1)
    n_ffn_pre = (norm_ffn_pre_l0, norm_ffn_pre_l1)
    n_ffn_post = (norm_ffn_post_l0, norm_ffn_post_l1)
    ffn = ((ffn_up_l0, ffn_conv_w_l0, ffn_conv_b_l0, ffn_down_l0),
           (ffn_up_l1, ffn_conv_w_l1, ffn_conv_b_l1, ffn_down_l1))
    l0_mix = (mix_in_l0, mix_out_l0, ssd_conv_w, ssd_conv_b, ssd_dt_bias, ssd_A_log, ssd_D, ssd_norm_w,
              dn_conv_w, dn_dt_bias, dn_A_log, dn_norm_w)
    l1_mix = (mix_in_l1, mix_out_l1, sg_ln_w, sg_ln_b, sg_w_s, sg_b_s, attn_sink)

    xp, xs = x_prompt, x_sample
    for layer in range(DEPTH):
        mod_p = adaln(c_ctx[None, :], mod_w[layer], mod_b[layer])
        mod_s = adaln(c, mod_w[layer], mod_b[layer])
        hp = modulated_rmsnorm(xp, n_mix_pre[layer], mod_p[:, 0], mod_p[:, 1])
        hs = modulated_rmsnorm(xs, n_mix_pre[layer], mod_s[:, 0], mod_s[:, 1])
        if layer % 2 == 0:
            zero_ssd = jnp.zeros((xp.shape[0],) + state_l0_ssd.shape[1:], F32)
            zero_dn = jnp.zeros((xp.shape[0],) + state_l0_dn.shape[1:], F32)
            op, new_ssd, new_dn = ssd_delta_mixer(hp, zero_ssd, zero_dn, *l0_mix)
            os_, _, _ = ssd_delta_mixer(hs, state_l0_ssd, state_l0_dn, *l0_mix)
        else:
            op, new_k, new_v = l1_mixer_context(hp, *l1_mix)
            os_ = l1_mixer_latent(hs, cache_l1_k, cache_l1_v, *l1_mix)
        xp = xp + mod_p[:, 2, None, :] * rmsnorm(op, n_mix_post[layer])
        xs = xs + mod_s[:, 2, None, :] * rmsnorm(os_, n_mix_post[layer])
        hp = modulated_rmsnorm(xp, n_ffn_pre[layer], mod_p[:, 3], mod_p[:, 4])
        hs = modulated_rmsnorm(xs, n_ffn_pre[layer], mod_s[:, 3], mod_s[:, 4])
        xp = xp + mod_p[:, 5, None, :] * rmsnorm(conv_ffn(hp, *ffn[layer]), n_ffn_post[layer])
        xs = xs + mod_s[:, 5, None, :] * rmsnorm(conv_ffn(hs, *ffn[layer]), n_ffn_post[layer])
    return (xp, xs, new_ssd, new_dn, new_k, new_v)
```

```python
import functools

import jax
import jax.numpy as jnp
from jax import lax
from jax.experimental import pallas as pl
from jax.experimental.pallas import tpu as pltpu

F32 = jnp.float32
BF16 = jnp.bfloat16

D = 1024
N_PROMPT = 32
SEG = 256
CHUNK = 128
SEG_PER_SAMPLE = 8
N_SAMPLE = 2
NSEG = N_PROMPT + N_SAMPLE * SEG_PER_SAMPLE
T = NSEG * SEG
HALO = 16
EPS = 1e-6
VMEM_LIMIT = 56 * 1024 * 1024

SSD_HEADS = 16
DN_HEADS = 8
D_FF = 2816
FF_CHUNK = 256

CH_DT = 0
CH_BETA = 48
CH_ACS = 64
CH_GCS = 96


def _dot(a, b):
    return jnp.dot(a, b, preferred_element_type=F32)


def _dot_nt(a, b):
    return lax.dot_general(a, b, (((1,), (1,)), ((), ())), preferred_element_type=F32)


def _dot_tn(a, b):
    return lax.dot_general(a, b, (((0,), (0,)), ((), ())), preferred_element_type=F32)


def _silu(x):
    return x * jax.nn.sigmoid(x)


def _rms(x, w):
    return x * lax.rsqrt(jnp.mean(x * x, axis=-1, keepdims=True) + EPS) * w


def _mod_index(i):
    return jnp.where(i < N_PROMPT, 0, 1 + jnp.maximum(i - N_PROMPT, 0) // SEG_PER_SAMPLE)


def _halo_flags(i):
    r = lax.rem(jnp.maximum(i - N_PROMPT, 0), SEG_PER_SAMPLE)
    in_sample = i >= N_PROMPT
    has_prev = jnp.where(jnp.logical_and(in_sample, r != 0), 1.0, 0.0)
    has_next = jnp.where(jnp.logical_and(in_sample, r != SEG_PER_SAMPLE - 1), 1.0, 0.0)
    return has_prev, has_next


def _row_spec(width):
    return pl.BlockSpec((SEG, width), lambda i: (i, 0))


def _prev_spec():
    return pl.BlockSpec((HALO, D), lambda i: (jnp.maximum(i * (SEG // HALO) - 1, 0), 0))


def _next_spec():
    return pl.BlockSpec((HALO, D), lambda i: (jnp.minimum((i + 1) * (SEG // HALO), T // HALO - 1), 0))


def _mod_spec():
    return pl.BlockSpec((1, 8, D), lambda i: (_mod_index(i), 0, 0))


def _const_spec(shape):
    nd = len(shape)
    return pl.BlockSpec(shape, lambda *_: (0,) * nd, pipeline_mode=pl.Buffered(1))


def _params(sem):
    return pltpu.CompilerParams(dimension_semantics=sem, vmem_limit_bytes=VMEM_LIMIT)


def _adaln_kernel(c_ref, w_ref, b_ref, o_ref):
    s = _silu(c_ref[...]).astype(BF16)
    o_ref[...] = _dot(s, w_ref[...].astype(BF16)) + b_ref[...]


def _adaln(cond8, w, b):
    tn = 768
    out = pl.pallas_call(
        _adaln_kernel,
        grid=(6 * D // tn,),
        in_specs=[pl.BlockSpec((8, D), lambda j: (0, 0)),
                  pl.BlockSpec((D, tn), lambda j: (0, j)),
                  pl.BlockSpec((1, tn), lambda j: (0, j))],
        out_specs=pl.BlockSpec((8, tn), lambda j: (0, j)),
        out_shape=jax.ShapeDtypeStruct((8, 6 * D), F32),
        compiler_params=_params(("arbitrary",)),
        name="adaln",
    )(cond8, w, b.reshape(1, -1))
    mod = out.reshape(8, 6, D)[:3]
    return jnp.pad(mod, ((0, 0), (0, 2), (0, 0)))


L0_CONV = 5120
L0_COLS = 7296
CONV_CHUNK = 512


def _l0_in_kernel(x_ref, xp_ref, xn_ref, m_ref, nw_ref, w_ref, cw_ref, cb_ref,
                  xbc_ref, qkv_ref, z_ref, gate_ref, small_ref, pscr):
    i = pl.program_id(0)
    shift = m_ref[0, 0:1, :]
    scale = m_ref[0, 1:2, :]
    xe = jnp.concatenate([x_ref[...], xp_ref[...], xn_ref[...]], axis=0)
    he = (_rms(xe, nw_ref[...]) * (1.0 + scale) + shift).astype(BF16)
    has_prev, has_next = _halo_flags(i)
    for cc in range(L0_CONV // CONV_CHUNK):
        cs = slice(cc * CONV_CHUNK, (cc + 1) * CONV_CHUNK)
        p = _dot(he, w_ref[:, cs])
        pscr[0:8, :] = p[SEG + 8:SEG + 16] * has_prev
        pscr[8:SEG + 8, :] = p[0:SEG]
        pscr[SEG + 8:SEG + 16, :] = p[SEG + 16:SEG + 24] * has_next
        acc = pscr[6:SEG + 6, :] * cw_ref[0:1, cs] + cb_ref[:, cs]
        for k in range(1, 5):
            acc = acc + pscr[6 + k:SEG + 6 + k, :] * cw_ref[k:k + 1, cs]
        val = _silu(acc).astype(BF16)
        if cc < 4:
            xbc_ref[:, cs] = val
        else:
            for t in range(4):
                qkv_ref[(cc - 4) * 4 + t] = val[:, t * 128:(t + 1) * 128]
    hc = he[0:SEG]
    z_ref[...] = _dot(hc, w_ref[:, 5120:6144]).astype(BF16)
    gate_ref[...] = _dot(hc, w_ref[:, 6144:7168]).astype(BF16)
    small_ref[...] = _dot(hc, w_ref[:, 7168:7296])


def _l0_in(x, modt, nw, w_main, cw, cb):
    return pl.pallas_call(
        _l0_in_kernel,
        grid=(NSEG,),
        in_specs=[_row_spec(D), _prev_spec(), _next_spec(), _mod_spec(),
                  _const_spec((1, D)), _const_spec((D, L0_COLS)),
                  _const_spec((5, L0_CONV)), _const_spec((1, L0_CONV))],
        out_specs=[_row_spec(2048),
                   pl.BlockSpec((24, SEG, 128), lambda i: (0, i, 0)),
                   _row_spec(D), _row_spec(D), _row_spec(128)],
        out_shape=[jax.ShapeDtypeStruct((T, 2048), BF16),
                   jax.ShapeDtypeStruct((24, T, 128), BF16),
                   jax.ShapeDtypeStruct((T, D), BF16),
                   jax.ShapeDtypeStruct((T, D), BF16),
                   jax.ShapeDtypeStruct((T, 128), F32)],
        scratch_shapes=[pltpu.VMEM((SEG + 16, CONV_CHUNK), F32)],
        compiler_params=_params(("arbitrary",)),
        name="l0_in",
    )(x, x, x, modt, nw, w_main, cw, cb)


def _split3_dot(a, tri):
    a1 = a.astype(BF16)
    r1 = a - a1.astype(F32)
    a2 = r1.astype(BF16)
    a3 = (r1 - a2.astype(F32)).astype(BF16)
    return _dot(a1, tri) + _dot(a2, tri) + _dot(a3, tri)


def _prep_kernel(s_ref, bias_ref, mult_ref, rowf_ref, colf_ref):
    row = lax.broadcasted_iota(jnp.int32, (CHUNK, CHUNK), 0)
    col = lax.broadcasted_iota(jnp.int32, (CHUNK, CHUNK), 1)
    upper = jnp.where(row <= col, 1.0, 0.0).astype(BF16)
    lower = jnp.where(row >= col, 1.0, 0.0).astype(BF16)
    backward = ((row >= 16) & (row < 32)) | ((row >= 40) & (row < 48))
    for c in range(SEG // CHUNK):
        rs = slice(c * CHUNK, (c + 1) * CHUNK)
        raw_t = s_ref[rs, :].T
        xb = raw_t + bias_ref[...]
        sp = jnp.maximum(xb, 0.0) + jnp.log1p(jnp.exp(-jnp.abs(xb)))
        vals = jnp.where(row < 48, sp, jax.nn.sigmoid(raw_t))
        a = jnp.where(row < 48, sp * mult_ref[...], 0.0)
        cum = jnp.where(backward, _split3_dot(a, lower), _split3_dot(a, upper))
        out = jnp.concatenate([vals[0:64], cum[0:64]], axis=0)
        rowf_ref[c] = out
        colf_ref[rs, :] = out.T


def _prep(small, bias_col, mult_col):
    return pl.pallas_call(
        _prep_kernel,
        grid=(NSEG,),
        in_specs=[_row_spec(128), _const_spec((128, 1)), _const_spec((128, 1))],
        out_specs=[pl.BlockSpec((2, CHUNK, CHUNK), lambda i: (i, 0, 0)), _row_spec(128)],
        out_shape=[jax.ShapeDtypeStruct((T // CHUNK, CHUNK, CHUNK), F32),
                   jax.ShapeDtypeStruct((T, 128), F32)],
        compiler_params=_params(("arbitrary",)),
        name="scan_prep",
    )(small, bias_col, mult_col)


def _scan_seg(d, j):
    return j + d * (NSEG - 1 - 2 * j)


def _scan_batch(seg):
    return jnp.maximum(seg - N_PROMPT, 0) // SEG_PER_SAMPLE


def _scan_specs(width_spec):
    seg_map = lambda d, j: _scan_seg(d, j)
    return dict(
        rowf=pl.BlockSpec((2, CHUNK, CHUNK), lambda d, j: (seg_map(d, j), 0, 0)),
        colf=pl.BlockSpec((SEG, 128), lambda d, j: (seg_map(d, j), 0)),
        h0=pl.BlockSpec((1, 1, 1024, 128), lambda d, j: (_scan_batch(seg_map(d, j)), d, 0, 0)),
        ns=pl.BlockSpec((1, 1, 1024, 128),
                        lambda d, j: (jnp.minimum(seg_map(d, j), N_PROMPT - 1), d, 0, 0)),
    )


def _scan_init(seg, direction, st, h0_ref):
    is_prompt = seg < N_PROMPT
    r = lax.rem(jnp.maximum(seg - N_PROMPT, 0), SEG_PER_SAMPLE)
    first = 0 if direction == 0 else SEG_PER_SAMPLE - 1

    @pl.when(is_prompt)
    def _():
        st[...] = jnp.zeros_like(st)

    @pl.when(jnp.logical_and(jnp.logical_not(is_prompt), r == first))
    def _():
        st[...] = h0_ref[0, 0].reshape(st.shape)

    return is_prompt


def _ssd_body(direction, seg, xbc_ref, rowf_ref, colf_ref, h0_ref, drow_ref, y_ref, ns_ref, st):
    is_prompt = _scan_init(seg, direction, st, h0_ref)
    row = lax.broadcasted_iota(jnp.int32, (CHUNK, CHUNK), 0)
    col = lax.broadcasted_iota(jnp.int32, (CHUNK, CHUNK), 1)
    mask = (col <= row) if direction == 0 else (col >= row)
    lane = lax.broadcasted_iota(jnp.int32, (CHUNK, 256), 1)
    srow = lax.broadcasted_iota(jnp.int32, (256, CHUNK), 0)
    for kk in range(SEG // CHUNK):
        c = kk if direction == 0 else SEG // CHUNK - 1 - kk
        rs = slice(c * CHUNK, (c + 1) * CHUNK)
        cf = colf_ref[rs, :]
        rf = rowf_ref[c]
        for g in range(4):
            bg = xbc_ref[rs, 1024 + 128 * g:1152 + 128 * g]
            cg = xbc_ref[rs, 1536 + 128 * g:1664 + 128 * g]
            xg = xbc_ref[rs, 256 * g:256 * g + 256]
            xf = xg.astype(F32)
            gmat = _dot_nt(cg, bg)
            yd = jnp.zeros((CHUNK, 256), F32)
            e_in = jnp.zeros((CHUNK, 256), F32)
            w_out = jnp.zeros((CHUNK, 256), F32)
            dec = jnp.zeros((256, CHUNK), F32)
            for hh in range(4):
                h = 4 * g + hh
                ch_dt = CH_DT + 16 * direction + h
                ch_ac = CH_ACS + 16 * direction + h
                ac = cf[:, ch_ac:ch_ac + 1]
                ar = rf[ch_ac:ch_ac + 1, :]
                dt_r = rf[ch_dt:ch_dt + 1, :]
                dt_c = cf[:, ch_dt:ch_dt + 1]
                last = ar[:, CHUNK - 1:CHUNK] if direction == 0 else ar[:, 0:1]
                lmat = jnp.exp(jnp.where(mask, ac - ar, -jnp.inf))
                m = (gmat * lmat * dt_r).astype(BF16)
                head = (lane >= 64 * hh) & (lane < 64 * hh + 64)
                yd = yd + _dot(m, jnp.where(head, xf, 0.0).astype(BF16))
                e_in = jnp.where(head, jnp.exp(ac), e_in)
                w_out = jnp.where(head, dt_c * jnp.exp(last - ac), w_out)
                dec = jnp.where((srow >= 64 * hh) & (srow < 64 * hh + 64), jnp.exp(last), dec)
            sg = st[256 * g:256 * g + 256, :]
            y = yd + _dot_nt(cg, sg.astype(BF16)) * e_in
            if direction == 0:
                y = y + xf * drow_ref[:, 256 * g:256 * g + 256]
            y_ref[0, rs, 256 * g:256 * g + 256] = y.astype(BF16)
            st[256 * g:256 * g + 256, :] = sg * dec + _dot_tn((xf * w_out).astype(BF16), bg)

    @pl.when(is_prompt)
    def _():
        ns_ref[0, 0] = st[...]


def _ssd_kernel(xbc_ref, rowf_ref, colf_ref, h0_ref, drow_ref, y_ref, ns_ref, st):
    d = pl.program_id(0)
    seg = _scan_seg(d, pl.program_id(1))
    for direction in (0, 1):
        pl.when(d == direction)(functools.partial(
            _ssd_body, direction, seg, xbc_ref, rowf_ref, colf_ref, h0_ref, drow_ref, y_ref, ns_ref, st))


def _ssd(xbc, rowf, colf, h0, drow):
    sp = _scan_specs(None)
    return pl.pallas_call(
        _ssd_kernel,
        grid=(2, NSEG),
        in_specs=[pl.BlockSpec((SEG, 2048), lambda d, j: (_scan_seg(d, j), 0)),
                  sp["rowf"], sp["colf"], sp["h0"],
                  pl.BlockSpec((1, D), lambda d, j: (0, 0))],
        out_specs=[pl.BlockSpec((1, SEG, D), lambda d, j: (d, _scan_seg(d, j), 0)), sp["ns"]],
        out_shape=[jax.ShapeDtypeStruct((2, T, D), BF16),
                   jax.ShapeDtypeStruct((N_PROMPT, 2, 1024, 128), F32)],
        scratch_shapes=[pltpu.VMEM((1024, 128), F32)],
        compiler_params=_params(("arbitrary", "arbitrary")),
        name="ssd_scan",
    )(xbc, rowf, colf, h0, drow)


def _l2n(x):
    return x * lax.rsqrt(jnp.sum(x * x, axis=-1, keepdims=True) + EPS)


def _dn_body(direction, seg, qkv_ref, rowf_ref, colf_ref, s0_ref, o_ref, ns_ref, st):
    is_prompt = _scan_init(seg, direction, st, s0_ref)
    row = lax.broadcasted_iota(jnp.int32, (CHUNK, CHUNK), 0)
    col = lax.broadcasted_iota(jnp.int32, (CHUNK, CHUNK), 1)
    tri = (col <= row) if direction == 0 else (col >= row)
    strict = (col < row) if direction == 0 else (col > row)
    eye = jnp.where(row == col, 1.0, 0.0)

    def sibling(lvl):
        return ((row >> (lvl + 1)) == (col >> (lvl + 1))) & ((row >> lvl) != (col >> lvl))

    def head(h, carry):
        for kk in range(SEG // CHUNK):
            c = kk if direction == 0 else SEG // CHUNK - 1 - kk
            rs = slice(c * CHUNK, (c + 1) * CHUNK)
            q = qkv_ref[h, rs, :].astype(F32)
            k = qkv_ref[DN_HEADS + h, rs, :].astype(F32)
            v = qkv_ref[2 * DN_HEADS + h, rs, :].astype(F32)
            qn = _l2n(q) * (128 ** -0.5)
            kn = _l2n(k)
            cf = colf_ref[rs, :]
            ch_b = CH_BETA + 8 * direction + h
            ch_g = CH_GCS + 8 * direction + h
            beta = jnp.sum(jnp.where(col == ch_b, cf, 0.0), axis=1, keepdims=True)
            gc = jnp.sum(jnp.where(col == ch_g, cf, 0.0), axis=1, keepdims=True)
            gr = rowf_ref[c, pl.ds(ch_g, 1), :]
            glast = gr[:, CHUNK - 1:CHUNK] if direction == 0 else gr[:, 0:1]
            decay = jnp.exp(jnp.where(tri, gc - gr, -jnp.inf))
            kb = kn * beta
            knb = kn.astype(BF16)
            a = jnp.where(strict, _dot_nt(kb.astype(BF16), knb) * decay, 0.0)
            tm = eye - jnp.where(sibling(0), a, 0.0)
            for lvl in range(1, 7):
                tb = tm.astype(BF16)
                a_off = jnp.where(sibling(lvl), a, 0.0).astype(BF16)
                tm = tm - _dot(tb, _dot(a_off, tb).astype(BF16))
            rhs = jnp.concatenate([v * beta, kb * jnp.exp(gc)], axis=1).astype(BF16)
            uw = _dot(tm.astype(BF16), rhs)
            u = uw[:, 0:128]
            w = uw[:, 128:256]
            qk = _dot_nt(qn.astype(BF16), knb) * decay
            s = st[h]
            sb = s.astype(BF16)
            v_new = u - _dot(w.astype(BF16), sb)
            vb = v_new.astype(BF16)
            o = _dot((qn * jnp.exp(gc)).astype(BF16), sb) + _dot(qk.astype(BF16), vb)
            k_end = (kn * jnp.exp(glast - gc)).astype(BF16)
            st[h] = s * jnp.exp(glast) + _dot_tn(k_end, vb)
            o_ref[0, h, rs, :] = o.astype(BF16)
        return carry

    lax.fori_loop(0, DN_HEADS, head, 0)

    @pl.when(is_prompt)
    def _():
        ns_ref[0, 0] = st[...].reshape(1024, 128)


def _dn_kernel(qkv_ref, rowf_ref, colf_ref, s0_ref, o_ref, ns_ref, st):
    d = pl.program_id(0)
    seg = _scan_seg(d, pl.program_id(1))
    for direction in (0, 1):
        pl.when(d == direction)(functools.partial(
            _dn_body, direction, seg, qkv_ref, rowf_ref, colf_ref, s0_ref, o_ref, ns_ref, st))


def _dn(qkv, rowf, colf, s0):
    sp = _scan_specs(None)
    return pl.pallas_call(
        _dn_kernel,
        grid=(2, NSEG),
        in_specs=[pl.BlockSpec((24, SEG, 128), lambda d, j: (0, _scan_seg(d, j), 0)),
                  sp["rowf"], sp["colf"], sp["h0"]],
        out_specs=[pl.BlockSpec((1, DN_HEADS, SEG, 128), lambda d, j: (d, 0, _scan_seg(d, j), 0)),
                   sp["ns"]],
        out_shape=[jax.ShapeDtypeStruct((2, DN_HEADS, T, 128), BF16),
                   jax.ShapeDtypeStruct((N_PROMPT, 2, 1024, 128), F32)],
        scratch_shapes=[pltpu.VMEM((DN_HEADS, 128, 128), F32)],
        compiler_params=_params(("arbitrary", "arbitrary")),
        name="dn_scan",
    )(qkv, rowf, colf, s0)


def _residual(x_ref, m_ref, gate_row, out, pw_ref):
    gate = m_ref[0, gate_row:gate_row + 1, :]
    return x_ref[...] + gate * _rms(out, pw_ref[...])


def _l0_out_kernel(y_ref, z_ref, o_ref, gate_ref, x_ref, m_ref, w_ref, snw_ref, dnw_ref, pw_ref, xo_ref):
    y = (y_ref[0].astype(F32) + y_ref[1].astype(F32)) * _silu(z_ref[...].astype(F32))
    yn = _rms(y, snw_ref[...]).astype(BF16)
    heads = []
    for h in range(DN_HEADS):
        oh = o_ref[0, h].astype(F32) + o_ref[1, h].astype(F32)
        gh = gate_ref[:, 128 * h:128 * h + 128].astype(F32)
        heads.append((_rms(oh, dnw_ref[...]) * _silu(gh)).astype(BF16))
    on = jnp.concatenate(heads, axis=1)
    out = _dot(yn, w_ref[0:1024, :]) + _dot(on, w_ref[1024:2048, :])
    xo_ref[...] = _residual(x_ref, m_ref, 2, out, pw_ref)


def _l0_out(y, z, o, gate, x, modt, w_out, snw, dnw, pw):
    return pl.pallas_call(
        _l0_out_kernel,
        grid=(NSEG,),
        in_specs=[pl.BlockSpec((2, SEG, D), lambda i: (0, i, 0)), _row_spec(D),
                  pl.BlockSpec((2, DN_HEADS, SEG, 128), lambda i: (0, 0, i, 0)), _row_spec(D),
                  _row_spec(D), _mod_spec(), _const_spec((2048, D)),
                  _const_spec((1, D)), _const_spec((1, 128)), _const_spec((1, D))],
        out_specs=_row_spec(D),
        out_shape=jax.ShapeDtypeStruct((T, D), F32),
        compiler_params=_params(("arbitrary",)),
        name="l0_out",
    )(y, z, o, gate, x, modt, w_out, snw, dnw, pw)


def _l1_out_kernel(a_ref, b_ref, x_ref, m_ref, w_ref, pw_ref, xo_ref):
    out = _dot(a_ref[...], w_ref[0:1024, :]) + _dot(b_ref[...], w_ref[1024:2048, :])
    xo_ref[...] = _residual(x_ref, m_ref, 2, out, pw_ref)


def _l1_out(a, b, x, modt, w_out, pw):
    return pl.pallas_call(
        _l1_out_kernel,
        grid=(NSEG,),
        in_specs=[_row_spec(D), _row_spec(D), _row_spec(D), _mod_spec(),
                  _const_spec((2048, D)), _const_spec((1, D))],
        out_specs=_row_spec(D),
        out_shape=jax.ShapeDtypeStruct((T, D), F32),
        compiler_params=_params(("arbitrary",)),
        name="l1_out",
    )(a, b, x, modt, w_out, pw)


def _ffn_kernel(x_ref, xp_ref, xn_ref, m_ref, nw_ref, wu_ref, cw_ref, cb_ref, wd_ref, pw_ref,
                xo_ref, pscr):
    i = pl.program_id(0)
    shift = m_ref[0, 3:4, :]
    scale = m_ref[0, 4:5, :]
    xe = jnp.concatenate([x_ref[...], xp_ref[...], xn_ref[...]], axis=0)
    he = (_rms(xe, nw_ref[...]) * (1.0 + scale) + shift).astype(BF16)
    has_prev, has_next = _halo_flags(i)
    acc = jnp.zeros((SEG, D), F32)
    for cc in range(D_FF // FF_CHUNK):
        halves = []
        for base in (0, D_FF):
            cs = slice(base + cc * FF_CHUNK, base + (cc + 1) * FF_CHUNK)
            p = _dot(he, wu_ref[:, cs])
            pscr[0:8, :] = p[SEG + 8:SEG + 16] * has_prev
            pscr[8:SEG + 8, :] = p[0:SEG]
            pscr[SEG + 8:SEG + 16, :] = p[SEG + 16:SEG + 24] * has_next
            t = pscr[7:SEG + 7, :] * cw_ref[0:1, cs] + cb_ref[:, cs]
            for k in range(1, 3):
                t = t + pscr[7 + k:SEG + 7 + k, :] * cw_ref[k:k + 1, cs]
            halves.append(t)
        g = (_silu(halves[0]) * halves[1]).astype(BF16)
        acc = acc + _dot(g, wd_ref[cc * FF_CHUNK:(cc + 1) * FF_CHUNK, :])
    xo_ref[...] = _residual(x_ref, m_ref, 5, acc, pw_ref)


def _ffn(x, modt, nw, wu, cw, cb, wd, pw):
    return pl.pallas_call(
        _ffn_kernel,
        grid=(NSEG,),
        in_specs=[_row_spec(D), _prev_spec(), _next_spec(), _mod_spec(), _const_spec((1, D)),
                  _const_spec((D, 2 * D_FF)), _const_spec((3, 2 * D_FF)), _const_spec((1, 2 * D_FF)),
                  _const_spec((D_FF, D)), _const_spec((1, D))],
        out_specs=_row_spec(D),
        out_shape=jax.ShapeDtypeStruct((T, D), F32),
        scratch_shapes=[pltpu.VMEM((SEG + 16, FF_CHUNK), F32)],
        compiler_params=_params(("arbitrary",)),
        name="conv_ffn",
    )(x, x, x, modt, nw, wu, cw, cb, wd, pw)


def _l1_in_kernel(x_ref, m_ref, nw_ref, w_ref, lnw_ref, lnb_ref, ws_ref, bs_ref,
                  mlp_ref, q_ref, k_ref, v_ref):
    shift = m_ref[0, 0:1, :]
    scale = m_ref[0, 1:2, :]
    h = (_rms(x_ref[...], nw_ref[...]) * (1.0 + scale) + shift).astype(BF16)
    u = jax.nn.gelu(_dot(h, w_ref[:, 0:1024]))
    gv = jax.nn.gelu(_dot(h, w_ref[:, 1024:2048]))
    gc = gv - jnp.mean(gv, axis=-1, keepdims=True)
    gn = gc * lax.rsqrt(jnp.mean(gc * gc, axis=-1, keepdims=True) + EPS) * lnw_ref[...] + lnb_ref[...]
    gb = gn.astype(BF16)
    for c in range(SEG // CHUNK):
        rs = slice(c * CHUNK, (c + 1) * CHUNK)
        for g in range(4):
            cs = slice(256 * g, 256 * g + 256)
            sv = _dot(ws_ref[g], gb[rs, cs]) + bs_ref[:, cs]
            mlp_ref[rs, cs] = (u[rs, cs] * sv).astype(BF16)
    q_ref[...] = _dot(h, w_ref[:, 2048:3072]).astype(BF16)
    k_ref[...] = _dot(h, w_ref[:, 3072:3328])
    v_ref[...] = _dot(h, w_ref[:, 3328:3584])


def _l1_in(x, modt, nw, w_in, lnw, lnb, ws, bs):
    return pl.pallas_call(
        _l1_in_kernel,
        grid=(NSEG,),
        in_specs=[_row_spec(D), _mod_spec(), _const_spec((1, D)), _const_spec((D, 3584)),
                  _const_spec((1, D)), _const_spec((1, D)), _const_spec((4, CHUNK, CHUNK)),
                  _const_spec((CHUNK, D))],
        out_specs=[_row_spec(D), _row_spec(D), _row_spec(256), _row_spec(256)],
        out_shape=[jax.ShapeDtypeStruct((T, D), BF16), jax.ShapeDtypeStruct((T, D), BF16),
                   jax.ShapeDtypeStruct((T, 256), F32), jax.ShapeDtypeStruct((T, 256), F32)],
        compiler_params=_params(("arbitrary",)),
        name="l1_in",
    )(x, modt, nw, w_in, lnw, lnb, ws, bs)


def _sink_softmax_pv(s, sink_col, vb):
    m = jnp.maximum(jnp.max(s, axis=-1, keepdims=True), sink_col)
    p = jnp.exp(s - m)
    den = jnp.sum(p, axis=-1, keepdims=True) + jnp.exp(sink_col - m)
    return _dot(p.astype(BF16), vb) / den


def _sink_column(sink_ref, kh, rows):
    parts = [jnp.broadcast_to(sink_ref[:, 4 * kh + g:4 * kh + g + 1], (rows, 1)) for g in range(4)]
    return jnp.concatenate(parts, axis=0)


def _ctx_attn_kernel(q_ref, k_ref, v_ref, sink_ref, o_ref):
    scale = 128 ** -0.5
    for kh in range(2):
        kb = k_ref[:, 128 * kh:128 * kh + 128].astype(BF16)
        vb = v_ref[:, 128 * kh:128 * kh + 128].astype(BF16)
        q4 = jnp.concatenate([q_ref[:, 128 * (4 * kh + g):128 * (4 * kh + g) + 128] for g in range(4)], axis=0)
        s = _dot_nt(q4, kb) * scale
        o = _sink_softmax_pv(s, _sink_column(sink_ref, kh, SEG), vb)
        for g in range(4):
            hd = 4 * kh + g
            o_ref[:, 128 * hd:128 * hd + 128] = o[g * SEG:(g + 1) * SEG].astype(BF16)


def _ctx_attn(q, k, v, sink_row):
    return pl.pallas_call(
        _ctx_attn_kernel,
        grid=(N_PROMPT,),
        in_specs=[_row_spec(D), _row_spec(256), _row_spec(256), _const_spec((1, 128))],
        out_specs=_row_spec(D),
        out_shape=jax.ShapeDtypeStruct((T, D), BF16),
        compiler_params=_params(("arbitrary",)),
        name="ctx_attn",
    )(q, k, v, sink_row)


N_QBLK = 2048 // CHUNK
N_CTX = 512


def _rope(x, cos2, sin2):
    return x * cos2 + pltpu.roll(x, 64, 1) * sin2


def _lat_attn_kernel(q_ref, kp_ref, kc_ref, kn_ref, vp_ref, vc_ref, vn_ref, ck_ref, cv_ref,
                     cosq_ref, sinq_ref, cosp_ref, sinp_ref, cosn_ref, sinn_ref, sink_ref,
                     prev_ref, o_ref):
    del prev_ref
    qi = pl.program_id(1)
    scale = 128 ** -0.5
    nk = 3 * CHUNK + N_CTX
    r = lax.broadcasted_iota(jnp.int32, (4 * CHUNK, nk), 0) & (CHUNK - 1)
    c = lax.broadcasted_iota(jnp.int32, (4 * CHUNK, nk), 1)
    far = 4 * CHUNK
    is_prev = c < CHUNK
    is_next = (c >= 2 * CHUNK) & (c < 3 * CHUNK)
    ok_prev = is_prev & (c >= r + jnp.where(qi > 0, 0, far))
    ok_next = is_next & (c - 2 * CHUNK <= r - jnp.where(qi < N_QBLK - 1, 0, far))
    mask = ok_prev | ok_next | jnp.logical_not(is_prev | is_next)
    for kh in range(2):
        ks = slice(128 * kh, 128 * kh + 128)
        k_all = jnp.concatenate([
            _rope(kp_ref[:, ks], cosp_ref[...], sinp_ref[...]).astype(BF16),
            _rope(kc_ref[:, ks], cosq_ref[...], sinq_ref[...]).astype(BF16),
            _rope(kn_ref[:, ks], cosn_ref[...], sinn_ref[...]).astype(BF16),
            ck_ref[0, :, ks].astype(BF16)], axis=0)
        v_all = jnp.concatenate([vp_ref[:, ks], vc_ref[:, ks], vn_ref[:, ks], cv_ref[0, :, ks]],
                                axis=0).astype(BF16)
        q4 = jnp.concatenate([
            (_rope(q_ref[:, 128 * (4 * kh + g):128 * (4 * kh + g) + 128].astype(F32),
                   cosq_ref[...], sinq_ref[...]) * scale).astype(BF16) for g in range(4)], axis=0)
        s = jnp.where(mask, _dot_nt(q4, k_all), -jnp.inf)
        o = _sink_softmax_pv(s, _sink_column(sink_ref, kh, CHUNK), v_all)
        for g in range(4):
            hd = 4 * kh + g
            o_ref[:, 128 * hd:128 * hd + 128] = o[g * CHUNK:(g + 1) * CHUNK].astype(BF16)


def _lat_attn(q, k, v, cache_k, cache_v, cos2, sin2, sink_row, attn_prev):
    base = N_PROMPT * SEG // CHUNK

    def blk(b, qi):
        return base + b * N_QBLK + qi

    def prev_blk(b, qi):
        return base + b * N_QBLK + jnp.maximum(qi - 1, 0)

    def next_blk(b, qi):
        return base + b * N_QBLK + jnp.minimum(qi + 1, N_QBLK - 1)

    kv_spec = lambda f: pl.BlockSpec((CHUNK, 256), lambda b, qi: (f(b, qi), 0))
    tab = lambda f: pl.BlockSpec((CHUNK, 128), lambda b, qi: (f(qi), 0))
    cache_spec = pl.BlockSpec((1, N_CTX, 256), lambda b, qi: (b, 0, 0))
    same = lambda qi: qi
    before = lambda qi: jnp.maximum(qi - 1, 0)
    after = lambda qi: jnp.minimum(qi + 1, N_QBLK - 1)
    return pl.pallas_call(
        _lat_attn_kernel,
        grid=(N_SAMPLE, N_QBLK),
        in_specs=[pl.BlockSpec((CHUNK, D), lambda b, qi: (blk(b, qi), 0)),
                  kv_spec(prev_blk), kv_spec(blk), kv_spec(next_blk),
                  kv_spec(prev_blk), kv_spec(blk), kv_spec(next_blk),
                  cache_spec, cache_spec,
                  tab(same), tab(same), tab(before), tab(before), tab(after), tab(after),
                  pl.BlockSpec((1, 128), lambda b, qi: (0, 0)),
                  pl.BlockSpec(memory_space=pl.ANY)],
        out_specs=pl.BlockSpec((CHUNK, D), lambda b, qi: (blk(b, qi), 0)),
        out_shape=jax.ShapeDtypeStruct((T, D), BF16),
        input_output_aliases={16: 0},
        compiler_params=_params(("arbitrary", "arbitrary")),
        name="lat_attn",
    )(q, k, k, k, v, v, v, cache_k, cache_v, cos2, sin2, cos2, sin2, cos2, sin2, sink_row, attn_prev)


def _rope_tables():
    length = 2048
    rows = length // 64
    rowp = jnp.repeat(jnp.arange(rows, dtype=F32), 64)
    colp = jnp.tile(jnp.arange(64, dtype=F32), rows)
    inv = 10000.0 ** (-jnp.arange(32, dtype=F32) / 32)
    ang = jnp.concatenate([rowp[:, None] * inv, colp[:, None] * inv], axis=-1)
    cos, sin = jnp.cos(ang), jnp.sin(ang)
    return jnp.concatenate([cos, cos], axis=-1), jnp.concatenate([-sin, sin], axis=-1)


def kernel(x_prompt, x_sample, state_l0_ssd, state_l0_dn, cache_l1_k, cache_l1_v, c, c_ctx, mod_w_l0, mod_b_l0, norm_mix_pre_l0, norm_mix_post_l0, norm_ffn_pre_l0, norm_ffn_post_l0, ffn_up_l0, ffn_conv_w_l0, ffn_conv_b_l0, ffn_down_l0, mod_w_l1, mod_b_l1, norm_mix_pre_l1, norm_mix_post_l1, norm_ffn_pre_l1, norm_ffn_post_l1, ffn_up_l1, ffn_conv_w_l1, ffn_conv_b_l1, ffn_down_l1, mix_in_l0, mix_out_l0, ssd_conv_w, ssd_conv_b, ssd_dt_bias, ssd_A_log, ssd_D, ssd_norm_w, dn_conv_w, dn_dt_bias, dn_A_log, dn_norm_w, mix_in_l1, mix_out_l1, sg_ln_w, sg_ln_b, sg_w_s, sg_b_s, attn_sink):
    row = lambda a: a.reshape(1, -1).astype(F32)
    x = jnp.concatenate([x_prompt.reshape(N_PROMPT * SEG, D), x_sample.reshape(N_SAMPLE * 2048, D)], axis=0)
    cond8 = jnp.concatenate([c_ctx[None, :], c, jnp.zeros((5, D), F32)], axis=0)
    mod0 = _adaln(cond8, mod_w_l0, mod_b_l0)
    mod1 = _adaln(cond8, mod_w_l1, mod_b_l1)

    w = mix_in_l0
    w_main = jnp.concatenate([w[:, 1024:3072], w[:, 3104:6176], w[:, 0:1024], w[:, 6176:7200],
                              w[:, 3072:3104], w[:, 7200:7232], jnp.zeros((D, 64), F32)], axis=1).astype(BF16)
    cw = jnp.concatenate([ssd_conv_w, dn_conv_w], axis=1)
    cb = jnp.concatenate([ssd_conv_b, jnp.zeros((3072,), F32)]).reshape(1, -1)
    xbc, qkv, z, gate, small = _l0_in(x, mod0, row(norm_mix_pre_l0), w_main, cw, cb)
    bias_col = jnp.concatenate([ssd_dt_bias.reshape(-1), dn_dt_bias.reshape(-1), jnp.zeros((80,), F32)]).reshape(128, 1)
    mult_col = jnp.concatenate([-jnp.exp(ssd_A_log.reshape(-1)), -jnp.exp(dn_A_log.reshape(-1)),
                                jnp.zeros((80,), F32)]).reshape(128, 1)
    rowf, colf = _prep(small, bias_col, mult_col)
    y, new_ssd = _ssd(xbc, rowf, colf, state_l0_ssd.reshape(N_SAMPLE, 2, 1024, 128),
                      jnp.repeat(ssd_D, 64).reshape(1, D))
    o, new_dn = _dn(qkv, rowf, colf, state_l0_dn.reshape(N_SAMPLE, 2, 1024, 128))
    x = _l0_out(y, z, o, gate, x, mod0, mix_out_l0.astype(BF16), row(ssd_norm_w), row(dn_norm_w),
                row(norm_mix_post_l0))
    x = _ffn(x, mod0, row(norm_ffn_pre_l0), ffn_up_l0.astype(BF16), ffn_conv_w_l0, row(ffn_conv_b_l0),
             ffn_down_l0.astype(BF16), row(norm_ffn_post_l0))

    bs_full = jnp.repeat(sg_b_s.T, 256, axis=1)
    mlp, q, k, v = _l1_in(x, mod1, row(norm_mix_pre_l1), mix_in_l1.astype(BF16), row(sg_ln_w), row(sg_ln_b),
                          sg_w_s.astype(BF16), bs_full)
    sink_row = jnp.pad(attn_sink, (0, 120)).reshape(1, 128)
    attn = _ctx_attn(q, k, v, sink_row)
    cos2, sin2 = _rope_tables()
    attn = _lat_attn(q, k, v, cache_l1_k.reshape(N_SAMPLE, N_CTX, 256), cache_l1_v.reshape(N_SAMPLE, N_CTX, 256),
                     cos2, sin2, sink_row, attn)
    x = _l1_out(mlp, attn, x, mod1, mix_out_l1.astype(BF16), row(norm_mix_post_l1))
    x = _ffn(x, mod1, row(norm_ffn_pre_l1), ffn_up_l1.astype(BF16), ffn_conv_w_l1, row(ffn_conv_b_l1),
             ffn_down_l1.astype(BF16), row(norm_ffn_post_l1))

    n_p = N_PROMPT * SEG
    return (x[:n_p].reshape(N_PROMPT, SEG, D),
            x[n_p:].reshape(N_SAMPLE, 2048, D),
            new_ssd.reshape(N_PROMPT, 2, SSD_HEADS, 64, 128),
            new_dn.reshape(N_PROMPT, 2, DN_HEADS, 128, 128),
            k[:n_p].reshape(N_PROMPT, SEG, 2, 128),
            v[:n_p].reshape(N_PROMPT, SEG, 2, 128))
```

```python
import functools

import jax
import jax.numpy as jnp
from jax import lax
from jax.experimental import pallas as pl
from jax.experimental.pallas import tpu as pltpu

F32 = jnp.float32
BF16 = jnp.bfloat16

D = 1024
N_PROMPT = 32
SEG = 256
CHUNK = 128
SEG_PER_SAMPLE = 8
N_SAMPLE = 2
NSEG = N_PROMPT + N_SAMPLE * SEG_PER_SAMPLE
T = NSEG * SEG
HALO = 16
EPS = 1e-6
VMEM_LIMIT = 56 * 1024 * 1024

SSD_HEADS = 16
DN_HEADS = 8
D_FF = 2816
FF_CHUNK = 256

CH_DT = 0
CH_BETA = 48
CH_ACS = 64
CH_GCS = 96


def _dot(a, b):
    return jnp.dot(a, b, preferred_element_type=F32)


def _dot_nt(a, b):
    return lax.dot_general(a, b, (((1,), (1,)), ((), ())), preferred_element_type=F32)


def _dot_tn(a, b):
    return lax.dot_general(a, b, (((0,), (0,)), ((), ())), preferred_element_type=F32)


def _silu(x):
    return x * jax.nn.sigmoid(x)


def _rms(x, w):
    return x * lax.rsqrt(jnp.mean(x * x, axis=-1, keepdims=True) + EPS) * w


def _mod_index(i):
    return jnp.where(i < N_PROMPT, 0, 1 + jnp.maximum(i - N_PROMPT, 0) // SEG_PER_SAMPLE)


def _halo_flags(i):
    r = lax.rem(jnp.maximum(i - N_PROMPT, 0), SEG_PER_SAMPLE)
    in_sample = i >= N_PROMPT
    has_prev = jnp.where(jnp.logical_and(in_sample, r != 0), 1.0, 0.0)
    has_next = jnp.where(jnp.logical_and(in_sample, r != SEG_PER_SAMPLE - 1), 1.0, 0.0)
    return has_prev, has_next


def _row_spec(width):
    return pl.BlockSpec((SEG, width), lambda i: (i, 0))


def _prev_spec():
    return pl.BlockSpec((HALO, D), lambda i: (jnp.maximum(i * (SEG // HALO) - 1, 0), 0))


def _next_spec():
    return pl.BlockSpec((HALO, D), lambda i: (jnp.minimum((i + 1) * (SEG // HALO), T // HALO - 1), 0))


def _mod_spec():
    return pl.BlockSpec((1, 8, D), lambda i: (_mod_index(i), 0, 0))


def _const_spec(shape):
    nd = len(shape)
    return pl.BlockSpec(shape, lambda *_: (0,) * nd, pipeline_mode=pl.Buffered(1))


def _params(sem):
    return pltpu.CompilerParams(dimension_semantics=sem, vmem_limit_bytes=VMEM_LIMIT)


def _adaln_kernel(c_ref, w_ref, b_ref, o_ref):
    s = _silu(c_ref[...]).astype(BF16)
    o_ref[...] = _dot(s, w_ref[...].astype(BF16)) + b_ref[...]


def _adaln(cond8, w, b):
    tn = 768
    out = pl.pallas_call(
        _adaln_kernel,
        grid=(6 * D // tn,),
        in_specs=[pl.BlockSpec((8, D), lambda j: (0, 0)),
                  pl.BlockSpec((D, tn), lambda j: (0, j)),
                  pl.BlockSpec((1, tn), lambda j: (0, j))],
        out_specs=pl.BlockSpec((8, tn), lambda j: (0, j)),
        out_shape=jax.ShapeDtypeStruct((8, 6 * D), F32),
        compiler_params=_params(("arbitrary",)),
        name="adaln",
    )(cond8, w, b.reshape(1, -1))
    mod = out.reshape(8, 6, D)[:3]
    return jnp.pad(mod, ((0, 0), (0, 2), (0, 0)))


L0_CONV = 5120
L0_COLS = 7296
CONV_CHUNK = 512


def _l0_in_kernel(x_ref, xp_ref, xn_ref, m_ref, nw_ref, w_ref, cw_ref, cb_ref,
                  xbc_ref, qkv_ref, z_ref, gate_ref, small_ref, pscr):
    i = pl.program_id(0)
    shift = m_ref[0, 0:1, :]
    scale = m_ref[0, 1:2, :]
    xe = jnp.concatenate([x_ref[...], xp_ref[...], xn_ref[...]], axis=0)
    he = (_rms(xe, nw_ref[...]) * (1.0 + scale) + shift).astype(BF16)
    has_prev, has_next = _halo_flags(i)
    for cc in range(L0_CONV // CONV_CHUNK):
        cs = slice(cc * CONV_CHUNK, (cc + 1) * CONV_CHUNK)
        p = _dot(he, w_ref[:, cs])
        pscr[0:8, :] = p[SEG + 8:SEG + 16] * has_prev
        pscr[8:SEG + 8, :] = p[0:SEG]
        pscr[SEG + 8:SEG + 16, :] = p[SEG + 16:SEG + 24] * has_next
        acc = pscr[6:SEG + 6, :] * cw_ref[0:1, cs] + cb_ref[:, cs]
        for k in range(1, 5):
            acc = acc + pscr[6 + k:SEG + 6 + k, :] * cw_ref[k:k + 1, cs]
        val = _silu(acc).astype(BF16)
        if cc < 4:
            xbc_ref[:, cs] = val
        else:
            for t in range(4):
                qkv_ref[(cc - 4) * 4 + t] = val[:, t * 128:(t + 1) * 128]
    hc = he[0:SEG]
    z_ref[...] = _dot(hc, w_ref[:, 5120:6144]).astype(BF16)
    gate_ref[...] = _dot(hc, w_ref[:, 6144:7168]).astype(BF16)
    small_ref[...] = _dot(hc, w_ref[:, 7168:7296])


def _l0_in(x, modt, nw, w_main, cw, cb):
    return pl.pallas_call(
        _l0_in_kernel,
        grid=(NSEG,),
        in_specs=[_row_spec(D), _prev_spec(), _next_spec(), _mod_spec(),
                  _const_spec((1, D)), _const_spec((D, L0_COLS)),
                  _const_spec((5, L0_CONV)), _const_spec((1, L0_CONV))],
        out_specs=[_row_spec(2048),
                   pl.BlockSpec((24, SEG, 128), lambda i: (0, i, 0)),
                   _row_spec(D), _row_spec(D), _row_spec(128)],
        out_shape=[jax.ShapeDtypeStruct((T, 2048), BF16),
                   jax.ShapeDtypeStruct((24, T, 128), BF16),
                   jax.ShapeDtypeStruct((T, D), BF16),
                   jax.ShapeDtypeStruct((T, D), BF16),
                   jax.ShapeDtypeStruct((T, 128), F32)],
        scratch_shapes=[pltpu.VMEM((SEG + 16, CONV_CHUNK), F32)],
        compiler_params=_params(("arbitrary",)),
        name="l0_in",
    )(x, x, x, modt, nw, w_main, cw, cb)


def _split3_dot(a, tri):
    a1 = a.astype(BF16)
    r1 = a - a1.astype(F32)
    a2 = r1.astype(BF16)
    a3 = (r1 - a2.astype(F32)).astype(BF16)
    return _dot(a1, tri) + _dot(a2, tri) + _dot(a3, tri)


def _prep_kernel(s_ref, bias_ref, mult_ref, rowf_ref, colf_ref):
    row = lax.broadcasted_iota(jnp.int32, (CHUNK, CHUNK), 0)
    col = lax.broadcasted_iota(jnp.int32, (CHUNK, CHUNK), 1)
    upper = jnp.where(row <= col, 1.0, 0.0).astype(BF16)
    lower = jnp.where(row >= col, 1.0, 0.0).astype(BF16)
    backward = ((row >= 16) & (row < 32)) | ((row >= 40) & (row < 48))
    for c in range(SEG // CHUNK):
        rs = slice(c * CHUNK, (c + 1) * CHUNK)
        raw_t = s_ref[rs, :].T
        xb = raw_t + bias_ref[...]
        sp = jnp.maximum(xb, 0.0) + jnp.log1p(jnp.exp(-jnp.abs(xb)))
        vals = jnp.where(row < 48, sp, jax.nn.sigmoid(raw_t))
        a = jnp.where(row < 48, sp * mult_ref[...], 0.0)
        cum = jnp.where(backward, _split3_dot(a, lower), _split3_dot(a, upper))
        out = jnp.concatenate([vals[0:64], cum[0:64]], axis=0)
        rowf_ref[c] = out
        colf_ref[rs, :] = out.T


def _prep(small, bias_col, mult_col):
    return pl.pallas_call(
        _prep_kernel,
        grid=(NSEG,),
        in_specs=[_row_spec(128), _const_spec((128, 1)), _const_spec((128, 1))],
        out_specs=[pl.BlockSpec((2, CHUNK, CHUNK), lambda i: (i, 0, 0)), _row_spec(128)],
        out_shape=[jax.ShapeDtypeStruct((T // CHUNK, CHUNK, CHUNK), F32),
                   jax.ShapeDtypeStruct((T, 128), F32)],
        compiler_params=_params(("arbitrary",)),
        name="scan_prep",
    )(small, bias_col, mult_col)


def _scan_seg(d, j):
    return j + d * (NSEG - 1 - 2 * j)


def _scan_batch(seg):
    return jnp.maximum(seg - N_PROMPT, 0) // SEG_PER_SAMPLE


def _scan_specs(width_spec):
    seg_map = lambda d, j: _scan_seg(d, j)
    return dict(
        rowf=pl.BlockSpec((2, CHUNK, CHUNK), lambda d, j: (seg_map(d, j), 0, 0)),
        colf=pl.BlockSpec((SEG, 128), lambda d, j: (seg_map(d, j), 0)),
        h0=pl.BlockSpec((1, 1, 1024, 128), lambda d, j: (_scan_batch(seg_map(d, j)), d, 0, 0)),
        ns=pl.BlockSpec((1, 1, 1024, 128),
                        lambda d, j: (jnp.minimum(seg_map(d, j), N_PROMPT - 1), d, 0, 0)),
    )


def _scan_init(seg, direction, st, h0_ref):
    is_prompt = seg < N_PROMPT
    r = lax.rem(jnp.maximum(seg - N_PROMPT, 0), SEG_PER_SAMPLE)
    first = 0 if direction == 0 else SEG_PER_SAMPLE - 1

    @pl.when(is_prompt)
    def _():
        st[...] = jnp.zeros_like(st)

    @pl.when(jnp.logical_and(jnp.logical_not(is_prompt), r == first))
    def _():
        st[...] = h0_ref[0, 0].reshape(st.shape)

    return is_prompt


def _ssd_body(direction, seg, xbc_ref, rowf_ref, colf_ref, h0_ref, drow_ref, y_ref, ns_ref, st):
    is_prompt = _scan_init(seg, direction, st, h0_ref)
    row = lax.broadcasted_iota(jnp.int32, (CHUNK, CHUNK), 0)
    col = lax.broadcasted_iota(jnp.int32, (CHUNK, CHUNK), 1)
    mask = (col <= row) if direction == 0 else (col >= row)
    lane = lax.broadcasted_iota(jnp.int32, (CHUNK, 256), 1)
    srow = lax.broadcasted_iota(jnp.int32, (256, CHUNK), 0)
    for kk in range(SEG // CHUNK):
        c = kk if direction == 0 else SEG // CHUNK - 1 - kk
        rs = slice(c * CHUNK, (c + 1) * CHUNK)
        cf = colf_ref[rs, :]
        rf = rowf_ref[c]
        for g in range(4):
            bg = xbc_ref[rs, 1024 + 128 * g:1152 + 128 * g]
            cg = xbc_ref[rs, 1536 + 128 * g:1664 + 128 * g]
            xg = xbc_ref[rs, 256 * g:256 * g + 256]
            xf = xg.astype(F32)
            gmat = _dot_nt(cg, bg)
            yd = jnp.zeros((CHUNK, 256), F32)
            e_in = jnp.zeros((CHUNK, 256), F32)
            w_out = jnp.zeros((CHUNK, 256), F32)
            dec = jnp.zeros((256, CHUNK), F32)
            for hh in range(4):
                h = 4 * g + hh
                ch_dt = CH_DT + 16 * direction + h
                ch_ac = CH_ACS + 16 * direction + h
                ac = cf[:, ch_ac:ch_ac + 1]
                ar = rf[ch_ac:ch_ac + 1, :]
                dt_r = rf[ch_dt:ch_dt + 1, :]
                dt_c = cf[:, ch_dt:ch_dt + 1]
                last = ar[:, CHUNK - 1:CHUNK] if direction == 0 else ar[:, 0:1]
                lmat = jnp.exp(jnp.where(mask, ac - ar, -jnp.inf))
                m = (gmat * lmat * dt_r).astype(BF16)
                head = (lane >= 64 * hh) & (lane < 64 * hh + 64)
                yd = yd + _dot(m, jnp.where(head, xf, 0.0).astype(BF16))
                e_in = jnp.where(head, jnp.exp(ac), e_in)
                w_out = jnp.where(head, dt_c * jnp.exp(last - ac), w_out)
                dec = jnp.where((srow >= 64 * hh) & (srow < 64 * hh + 64), jnp.exp(last), dec)
            sg = st[256 * g:256 * g + 256, :]
            y = yd + _dot_nt(cg, sg.astype(BF16)) * e_in
            if direction == 0:
                y = y + xf * drow_ref[:, 256 * g:256 * g + 256]
            y_ref[0, rs, 256 * g:256 * g + 256] = y.astype(BF16)
            st[256 * g:256 * g + 256, :] = sg * dec + _dot_tn((xf * w_out).astype(BF16), bg)

    @pl.when(is_prompt)
    def _():
        ns_ref[0, 0] = st[...]


def _ssd_kernel(xbc_ref, rowf_ref, colf_ref, h0_ref, drow_ref, y_ref, ns_ref, st):
    d = pl.program_id(0)
    seg = _scan_seg(d, pl.program_id(1))
    for direction in (0, 1):
        pl.when(d == direction)(functools.partial(
            _ssd_body, direction, seg, xbc_ref, rowf_ref, colf_ref, h0_ref, drow_ref, y_ref, ns_ref, st))


def _ssd(xbc, rowf, colf, h0, drow):
    sp = _scan_specs(None)
    return pl.pallas_call(
        _ssd_kernel,
        grid=(2, NSEG),
        in_specs=[pl.BlockSpec((SEG, 2048), lambda d, j: (_scan_seg(d, j), 0)),
                  sp["rowf"], sp["colf"], sp["h0"],
                  pl.BlockSpec((1, D), lambda d, j: (0, 0))],
        out_specs=[pl.BlockSpec((1, SEG, D), lambda d, j: (d, _scan_seg(d, j), 0)), sp["ns"]],
        out_shape=[jax.ShapeDtypeStruct((2, T, D), BF16),
                   jax.ShapeDtypeStruct((N_PROMPT, 2, 1024, 128), F32)],
        scratch_shapes=[pltpu.VMEM((1024, 128), F32)],
        compiler_params=_params(("arbitrary", "arbitrary")),
        name="ssd_scan",
    )(xbc, rowf, colf, h0, drow)


def _l2n(x):
    return x * lax.rsqrt(jnp.sum(x * x, axis=-1, keepdims=True) + EPS)


def _dn_body(direction, seg, qkv_ref, rowf_ref, colf_ref, s0_ref, lvl_ref, o_ref, ns_ref,
             st, a_s, d_s, rhs_s, qk_s, qd_s, ke_s, u_s, w_s, gl_s):
    is_prompt = _scan_init(seg, direction, st, s0_ref)
    row = lax.broadcasted_iota(jnp.int32, (CHUNK, CHUNK), 0)
    col = lax.broadcasted_iota(jnp.int32, (CHUNK, CHUNK), 1)
    tri = (col <= row) if direction == 0 else (col >= row)
    strict = (col < row) if direction == 0 else (col > row)
    eye = jnp.where(row == col, 1.0, 0.0)
    n_chunk = SEG // CHUNK
    order = range(n_chunk) if direction == 0 else range(n_chunk - 1, -1, -1)

    for h in range(DN_HEADS):
        ch_b = CH_BETA + 8 * direction + h
        ch_g = CH_GCS + 8 * direction + h
        for c in range(n_chunk):
            p = n_chunk * h + c
            rs = slice(c * CHUNK, (c + 1) * CHUNK)
            qn = _l2n(qkv_ref[h, rs, :].astype(F32)) * (128 ** -0.5)
            kn = _l2n(qkv_ref[DN_HEADS + h, rs, :].astype(F32))
            v = qkv_ref[2 * DN_HEADS + h, rs, :].astype(F32)
            beta = colf_ref[rs, ch_b:ch_b + 1]
            gc = colf_ref[rs, ch_g:ch_g + 1]
            gr = rowf_ref[c, ch_g:ch_g + 1, :]
            glast = gr[:, CHUNK - 1:CHUNK] if direction == 0 else gr[:, 0:1]
            decay = jnp.exp(jnp.where(tri, gc - gr, -jnp.inf))
            kb = kn * beta
            kq = _dot_nt(jnp.concatenate([kb, qn], axis=0).astype(BF16), kn.astype(BF16))
            a = jnp.where(strict, kq[0:CHUNK] * decay, 0.0)
            a_s[p] = a.astype(BF16)
            d_s[p] = eye - a * lvl_ref[0].astype(F32)
            qk_s[p] = (kq[CHUNK:2 * CHUNK] * decay).astype(BF16)
            eg = jnp.exp(gc)
            rhs_s[p] = jnp.concatenate([v * beta, kb * eg], axis=1).astype(BF16)
            qd_s[p] = (qn * eg).astype(BF16)
            ke_s[p] = (kn * jnp.exp(glast - gc)).astype(BF16)
            gl_s[p] = jnp.broadcast_to(jnp.exp(glast), (8, CHUNK))

    for lvl in range(1, 7):
        for p in range(n_chunk * DN_HEADS):
            tm = d_s[p]
            tb = tm.astype(BF16)
            a_off = a_s[p] * lvl_ref[lvl]
            d_s[p] = tm - _dot(tb, _dot(a_off, tb).astype(BF16))

    for p in range(n_chunk * DN_HEADS):
        uw = _dot(d_s[p].astype(BF16), rhs_s[p])
        u_s[p] = uw[:, 0:CHUNK]
        w_s[p] = uw[:, CHUNK:2 * CHUNK].astype(BF16)

    for h in range(DN_HEADS):
        s = st[h]
        for c in order:
            p = n_chunk * h + c
            sb = s.astype(BF16)
            wq = _dot(jnp.concatenate([w_s[p], qd_s[p]], axis=0), sb)
            vb = (u_s[p] - wq[0:CHUNK]).astype(BF16)
            o = wq[CHUNK:2 * CHUNK] + _dot(qk_s[p], vb)
            s = s * gl_s[p, 0:1, :] + _dot_tn(ke_s[p], vb)
            o_ref[0, h, c * CHUNK:(c + 1) * CHUNK, :] = o.astype(BF16)
        st[h] = s

    @pl.when(is_prompt)
    def _():
        ns_ref[0, 0] = st[...].reshape(1024, 128)


def _dn_kernel(qkv_ref, rowf_ref, colf_ref, s0_ref, lvl_ref, o_ref, ns_ref, *scratch):
    d = pl.program_id(0)
    seg = _scan_seg(d, pl.program_id(1))
    for direction in (0, 1):
        pl.when(d == direction)(functools.partial(
            _dn_body, direction, seg, qkv_ref, rowf_ref, colf_ref, s0_ref, lvl_ref, o_ref, ns_ref, *scratch))


def _sibling_masks():
    i = jnp.arange(CHUNK)[:, None]
    j = jnp.arange(CHUNK)[None, :]
    lv = [((i >> (l + 1)) == (j >> (l + 1))) & ((i >> l) != (j >> l)) for l in range(7)]
    return jnp.stack(lv).astype(BF16)


def _dn(qkv, rowf, colf, s0):
    sp = _scan_specs(None)
    n_prob = DN_HEADS * SEG // CHUNK
    mat = lambda dt: pltpu.VMEM((n_prob, CHUNK, CHUNK), dt)
    return pl.pallas_call(
        _dn_kernel,
        grid=(2, NSEG),
        in_specs=[pl.BlockSpec((24, SEG, 128), lambda d, j: (0, _scan_seg(d, j), 0)),
                  sp["rowf"], sp["colf"], sp["h0"], _const_spec((7, CHUNK, CHUNK))],
        out_specs=[pl.BlockSpec((1, DN_HEADS, SEG, 128), lambda d, j: (d, 0, _scan_seg(d, j), 0)),
                   sp["ns"]],
        out_shape=[jax.ShapeDtypeStruct((2, DN_HEADS, T, 128), BF16),
                   jax.ShapeDtypeStruct((N_PROMPT, 2, 1024, 128), F32)],
        scratch_shapes=[pltpu.VMEM((DN_HEADS, CHUNK, CHUNK), F32),
                        mat(BF16), mat(F32),
                        pltpu.VMEM((n_prob, CHUNK, 2 * CHUNK), BF16),
                        mat(BF16), mat(BF16), mat(BF16),
                        mat(F32), mat(BF16),
                        pltpu.VMEM((n_prob, 8, CHUNK), F32)],
        compiler_params=_params(("arbitrary", "arbitrary")),
        name="dn_scan",
    )(qkv, rowf, colf, s0, _sibling_masks())


def _residual(x_ref, m_ref, gate_row, out, pw_ref):
    gate = m_ref[0, gate_row:gate_row + 1, :]
    return x_ref[...] + gate * _rms(out, pw_ref[...])


def _l0_out_kernel(y_ref, z_ref, o_ref, gate_ref, x_ref, m_ref, w_ref, snw_ref, dnw_ref, pw_ref, xo_ref):
    y = (y_ref[0].astype(F32) + y_ref[1].astype(F32)) * _silu(z_ref[...].astype(F32))
    yn = _rms(y, snw_ref[...]).astype(BF16)
    heads = []
    for h in range(DN_HEADS):
        oh = o_ref[0, h].astype(F32) + o_ref[1, h].astype(F32)
        gh = gate_ref[:, 128 * h:128 * h + 128].astype(F32)
        heads.append((_rms(oh, dnw_ref[...]) * _silu(gh)).astype(BF16))
    on = jnp.concatenate(heads, axis=1)
    out = _dot(yn, w_ref[0:1024, :]) + _dot(on, w_ref[1024:2048, :])
    xo_ref[...] = _residual(x_ref, m_ref, 2, out, pw_ref)


def _l0_out(y, z, o, gate, x, modt, w_out, snw, dnw, pw):
    return pl.pallas_call(
        _l0_out_kernel,
        grid=(NSEG,),
        in_specs=[pl.BlockSpec((2, SEG, D), lambda i: (0, i, 0)), _row_spec(D),
                  pl.BlockSpec((2, DN_HEADS, SEG, 128), lambda i: (0, 0, i, 0)), _row_spec(D),
                  _row_spec(D), _mod_spec(), _const_spec((2048, D)),
                  _const_spec((1, D)), _const_spec((1, 128)), _const_spec((1, D))],
        out_specs=_row_spec(D),
        out_shape=jax.ShapeDtypeStruct((T, D), F32),
        compiler_params=_params(("arbitrary",)),
        name="l0_out",
    )(y, z, o, gate, x, modt, w_out, snw, dnw, pw)


def _l1_out_kernel(a_ref, b_ref, x_ref, m_ref, w_ref, pw_ref, xo_ref):
    out = _dot(a_ref[...], w_ref[0:1024, :]) + _dot(b_ref[...], w_ref[1024:2048, :])
    xo_ref[...] = _residual(x_ref, m_ref, 2, out, pw_ref)


def _l1_out(a, b, x, modt, w_out, pw):
    return pl.pallas_call(
        _l1_out_kernel,
        grid=(NSEG,),
        in_specs=[_row_spec(D), _row_spec(D), _row_spec(D), _mod_spec(),
                  _const_spec((2048, D)), _const_spec((1, D))],
        out_specs=_row_spec(D),
        out_shape=jax.ShapeDtypeStruct((T, D), F32),
        compiler_params=_params(("arbitrary",)),
        name="l1_out",
    )(a, b, x, modt, w_out, pw)


def _ffn_kernel(x_ref, xp_ref, xn_ref, m_ref, nw_ref, wu_ref, cw_ref, cb_ref, wd_ref, pw_ref,
                xo_ref, pscr):
    i = pl.program_id(0)
    shift = m_ref[0, 3:4, :]
    scale = m_ref[0, 4:5, :]
    xe = jnp.concatenate([x_ref[...], xp_ref[...], xn_ref[...]], axis=0)
    he = (_rms(xe, nw_ref[...]) * (1.0 + scale) + shift).astype(BF16)
    has_prev, has_next = _halo_flags(i)
    acc = jnp.zeros((SEG, D), F32)
    for cc in range(D_FF // FF_CHUNK):
        halves = []
        for base in (0, D_FF):
            cs = slice(base + cc * FF_CHUNK, base + (cc + 1) * FF_CHUNK)
            p = _dot(he, wu_ref[:, cs])
            pscr[0:8, :] = p[SEG + 8:SEG + 16] * has_prev
            pscr[8:SEG + 8, :] = p[0:SEG]
            pscr[SEG + 8:SEG + 16, :] = p[SEG + 16:SEG + 24] * has_next
            t = pscr[7:SEG + 7, :] * cw_ref[0:1, cs] + cb_ref[:, cs]
            for k in range(1, 3):
                t = t + pscr[7 + k:SEG + 7 + k, :] * cw_ref[k:k + 1, cs]
            halves.append(t)
        g = (_silu(halves[0]) * halves[1]).astype(BF16)
        acc = acc + _dot(g, wd_ref[cc * FF_CHUNK:(cc + 1) * FF_CHUNK, :])
    xo_ref[...] = _residual(x_ref, m_ref, 5, acc, pw_ref)


def _ffn(x, modt, nw, wu, cw, cb, wd, pw):
    return pl.pallas_call(
        _ffn_kernel,
        grid=(NSEG,),
        in_specs=[_row_spec(D), _prev_spec(), _next_spec(), _mod_spec(), _const_spec((1, D)),
                  _const_spec((D, 2 * D_FF)), _const_spec((3, 2 * D_FF)), _const_spec((1, 2 * D_FF)),
                  _const_spec((D_FF, D)), _const_spec((1, D))],
        out_specs=_row_spec(D),
        out_shape=jax.ShapeDtypeStruct((T, D), F32),
        scratch_shapes=[pltpu.VMEM((SEG + 16, FF_CHUNK), F32)],
        compiler_params=_params(("arbitrary",)),
        name="conv_ffn",
    )(x, x, x, modt, nw, wu, cw, cb, wd, pw)


def _l1_in_kernel(x_ref, m_ref, nw_ref, w_ref, lnw_ref, lnb_ref, ws_ref, bs_ref,
                  mlp_ref, q_ref, k_ref, v_ref):
    shift = m_ref[0, 0:1, :]
    scale = m_ref[0, 1:2, :]
    h = (_rms(x_ref[...], nw_ref[...]) * (1.0 + scale) + shift).astype(BF16)
    u = jax.nn.gelu(_dot(h, w_ref[:, 0:1024]))
    gv = jax.nn.gelu(_dot(h, w_ref[:, 1024:2048]))
    gc = gv - jnp.mean(gv, axis=-1, keepdims=True)
    gn = gc * lax.rsqrt(jnp.mean(gc * gc, axis=-1, keepdims=True) + EPS) * lnw_ref[...] + lnb_ref[...]
    gb = gn.astype(BF16)
    for c in range(SEG // CHUNK):
        rs = slice(c * CHUNK, (c + 1) * CHUNK)
        for g in range(4):
            cs = slice(256 * g, 256 * g + 256)
            sv = _dot(ws_ref[g], gb[rs, cs]) + bs_ref[:, cs]
            mlp_ref[rs, cs] = (u[rs, cs] * sv).astype(BF16)
    q_ref[...] = _dot(h, w_ref[:, 2048:3072]).astype(BF16)
    k_ref[...] = _dot(h, w_ref[:, 3072:3328])
    v_ref[...] = _dot(h, w_ref[:, 3328:3584])


def _l1_in(x, modt, nw, w_in, lnw, lnb, ws, bs):
    return pl.pallas_call(
        _l1_in_kernel,
        grid=(NSEG,),
        in_specs=[_row_spec(D), _mod_spec(), _const_spec((1, D)), _const_spec((D, 3584)),
                  _const_spec((1, D)), _const_spec((1, D)), _const_spec((4, CHUNK, CHUNK)),
                  _const_spec((CHUNK, D))],
        out_specs=[_row_spec(D), _row_spec(D), _row_spec(256), _row_spec(256)],
        out_shape=[jax.ShapeDtypeStruct((T, D), BF16), jax.ShapeDtypeStruct((T, D), BF16),
                   jax.ShapeDtypeStruct((T, 256), F32), jax.ShapeDtypeStruct((T, 256), F32)],
        compiler_params=_params(("arbitrary",)),
        name="l1_in",
    )(x, modt, nw, w_in, lnw, lnb, ws, bs)


def _sink_softmax_pv(s, sink_col, vb):
    m = jnp.maximum(jnp.max(s, axis=-1, keepdims=True), sink_col)
    p = jnp.exp(s - m)
    den = jnp.sum(p, axis=-1, keepdims=True) + jnp.exp(sink_col - m)
    return _dot(p.astype(BF16), vb) / den


def _sink_column(sink_ref, kh, rows):
    parts = [jnp.broadcast_to(sink_ref[:, 4 * kh + g:4 * kh + g + 1], (rows, 1)) for g in range(4)]
    return jnp.concatenate(parts, axis=0)


def _ctx_attn_kernel(q_ref, k_ref, v_ref, sink_ref, o_ref):
    scale = 128 ** -0.5
    for kh in range(2):
        kb = k_ref[:, 128 * kh:128 * kh + 128].astype(BF16)
        vb = v_ref[:, 128 * kh:128 * kh + 128].astype(BF16)
        q4 = jnp.concatenate([q_ref[:, 128 * (4 * kh + g):128 * (4 * kh + g) + 128] for g in range(4)], axis=0)
        s = _dot_nt(q4, kb) * scale
        o = _sink_softmax_pv(s, _sink_column(sink_ref, kh, SEG), vb)
        for g in range(4):
            hd = 4 * kh + g
            o_ref[:, 128 * hd:128 * hd + 128] = o[g * SEG:(g + 1) * SEG].astype(BF16)


def _ctx_attn(q, k, v, sink_row):
    return pl.pallas_call(
        _ctx_attn_kernel,
        grid=(N_PROMPT,),
        in_specs=[_row_spec(D), _row_spec(256), _row_spec(256), _const_spec((1, 128))],
        out_specs=_row_spec(D),
        out_shape=jax.ShapeDtypeStruct((T, D), BF16),
        compiler_params=_params(("arbitrary",)),
        name="ctx_attn",
    )(q, k, v, sink_row)


N_QBLK = 2048 // CHUNK
N_CTX = 512


def _rope(x, cos2, sin2):
    return x * cos2 + pltpu.roll(x, 64, 1) * sin2


def _lat_attn_kernel(q_ref, kp_ref, kc_ref, kn_ref, vp_ref, vc_ref, vn_ref, ck_ref, cv_ref,
                     cosq_ref, sinq_ref, cosp_ref, sinp_ref, cosn_ref, sinn_ref, sink_ref,
                     prev_ref, o_ref):
    del prev_ref
    qi = pl.program_id(1)
    scale = 128 ** -0.5
    nk = 3 * CHUNK + N_CTX
    r = lax.broadcasted_iota(jnp.int32, (4 * CHUNK, nk), 0) & (CHUNK - 1)
    c = lax.broadcasted_iota(jnp.int32, (4 * CHUNK, nk), 1)
    far = 4 * CHUNK
    is_prev = c < CHUNK
    is_next = (c >= 2 * CHUNK) & (c < 3 * CHUNK)
    ok_prev = is_prev & (c >= r + jnp.where(qi > 0, 0, far))
    ok_next = is_next & (c - 2 * CHUNK <= r - jnp.where(qi < N_QBLK - 1, 0, far))
    mask = ok_prev | ok_next | jnp.logical_not(is_prev | is_next)
    for kh in range(2):
        ks = slice(128 * kh, 128 * kh + 128)
        k_all = jnp.concatenate([
            _rope(kp_ref[:, ks], cosp_ref[...], sinp_ref[...]).astype(BF16),
            _rope(kc_ref[:, ks], cosq_ref[...], sinq_ref[...]).astype(BF16),
            _rope(kn_ref[:, ks], cosn_ref[...], sinn_ref[...]).astype(BF16),
            ck_ref[0, :, ks].astype(BF16)], axis=0)
        v_all = jnp.concatenate([vp_ref[:, ks], vc_ref[:, ks], vn_ref[:, ks], cv_ref[0, :, ks]],
                                axis=0).astype(BF16)
        q4 = jnp.concatenate([
            (_rope(q_ref[:, 128 * (4 * kh + g):128 * (4 * kh + g) + 128].astype(F32),
                   cosq_ref[...], sinq_ref[...]) * scale).astype(BF16) for g in range(4)], axis=0)
        s = jnp.where(mask, _dot_nt(q4, k_all), -jnp.inf)
        o = _sink_softmax_pv(s, _sink_column(sink_ref, kh, CHUNK), v_all)
        for g in range(4):
            hd = 4 * kh + g
            o_ref[:, 128 * hd:128 * hd + 128] = o[g * CHUNK:(g + 1) * CHUNK].astype(BF16)


def _lat_attn(q, k, v, cache_k, cache_v, cos2, sin2, sink_row, attn_prev):
    base = N_PROMPT * SEG // CHUNK

    def blk(b, qi):
        return base + b * N_QBLK + qi

    def prev_blk(b, qi):
        return base + b * N_QBLK + jnp.maximum(qi - 1, 0)

    def next_blk(b, qi):
        return base + b * N_QBLK + jnp.minimum(qi + 1, N_QBLK - 1)

    kv_spec = lambda f: pl.BlockSpec((CHUNK, 256), lambda b, qi: (f(b, qi), 0))
    tab = lambda f: pl.BlockSpec((CHUNK, 128), lambda b, qi: (f(qi), 0))
    cache_spec = pl.BlockSpec((1, N_CTX, 256), lambda b, qi: (b, 0, 0))
    same = lambda qi: qi
    before = lambda qi: jnp.maximum(qi - 1, 0)
    after = lambda qi: jnp.minimum(qi + 1, N_QBLK - 1)
    return pl.pallas_call(
        _lat_attn_kernel,
        grid=(N_SAMPLE, N_QBLK),
        in_specs=[pl.BlockSpec((CHUNK, D), lambda b, qi: (blk(b, qi), 0)),
                  kv_spec(prev_blk), kv_spec(blk), kv_spec(next_blk),
                  kv_spec(prev_blk), kv_spec(blk), kv_spec(next_blk),
                  cache_spec, cache_spec,
                  tab(same), tab(same), tab(before), tab(before), tab(after), tab(after),
                  pl.BlockSpec((1, 128), lambda b, qi: (0, 0)),
                  pl.BlockSpec(memory_space=pl.ANY)],
        out_specs=pl.BlockSpec((CHUNK, D), lambda b, qi: (blk(b, qi), 0)),
        out_shape=jax.ShapeDtypeStruct((T, D), BF16),
        input_output_aliases={16: 0},
        compiler_params=_params(("arbitrary", "arbitrary")),
        name="lat_attn",
    )(q, k, k, k, v, v, v, cache_k, cache_v, cos2, sin2, cos2, sin2, cos2, sin2, sink_row, attn_prev)


def _rope_tables():
    length = 2048
    rows = length // 64
    rowp = jnp.repeat(jnp.arange(rows, dtype=F32), 64)
    colp = jnp.tile(jnp.arange(64, dtype=F32), rows)
    inv = 10000.0 ** (-jnp.arange(32, dtype=F32) / 32)
    ang = jnp.concatenate([rowp[:, None] * inv, colp[:, None] * inv], axis=-1)
    cos, sin = jnp.cos(ang), jnp.sin(ang)
    return jnp.concatenate([cos, cos], axis=-1), jnp.concatenate([-sin, sin], axis=-1)


def kernel(x_prompt, x_sample, state_l0_ssd, state_l0_dn, cache_l1_k, cache_l1_v, c, c_ctx, mod_w_l0, mod_b_l0, norm_mix_pre_l0, norm_mix_post_l0, norm_ffn_pre_l0, norm_ffn_post_l0, ffn_up_l0, ffn_conv_w_l0, ffn_conv_b_l0, ffn_down_l0, mod_w_l1, mod_b_l1, norm_mix_pre_l1, norm_mix_post_l1, norm_ffn_pre_l1, norm_ffn_post_l1, ffn_up_l1, ffn_conv_w_l1, ffn_conv_b_l1, ffn_down_l1, mix_in_l0, mix_out_l0, ssd_conv_w, ssd_conv_b, ssd_dt_bias, ssd_A_log, ssd_D, ssd_norm_w, dn_conv_w, dn_dt_bias, dn_A_log, dn_norm_w, mix_in_l1, mix_out_l1, sg_ln_w, sg_ln_b, sg_w_s, sg_b_s, attn_sink):
    row = lambda a: a.reshape(1, -1).astype(F32)
    x = jnp.concatenate([x_prompt.reshape(N_PROMPT * SEG, D), x_sample.reshape(N_SAMPLE * 2048, D)], axis=0)
    cond8 = jnp.concatenate([c_ctx[None, :], c, jnp.zeros((5, D), F32)], axis=0)
    mod0 = _adaln(cond8, mod_w_l0, mod_b_l0)
    mod1 = _adaln(cond8, mod_w_l1, mod_b_l1)

    w = mix_in_l0
    w_main = jnp.concatenate([w[:, 1024:3072], w[:, 3104:6176], w[:, 0:1024], w[:, 6176:7200],
                              w[:, 3072:3104], w[:, 7200:7232], jnp.zeros((D, 64), F32)], axis=1).astype(BF16)
    cw = jnp.concatenate([ssd_conv_w, dn_conv_w], axis=1)
    cb = jnp.concatenate([ssd_conv_b, jnp.zeros((3072,), F32)]).reshape(1, -1)
    xbc, qkv, z, gate, small = _l0_in(x, mod0, row(norm_mix_pre_l0), w_main, cw, cb)
    bias_col = jnp.concatenate([ssd_dt_bias.reshape(-1), dn_dt_bias.reshape(-1), jnp.zeros((80,), F32)]).reshape(128, 1)
    mult_col = jnp.concatenate([-jnp.exp(ssd_A_log.reshape(-1)), -jnp.exp(dn_A_log.reshape(-1)),
                                jnp.zeros((80,), F32)]).reshape(128, 1)
    rowf, colf = _prep(small, bias_col, mult_col)
    y, new_ssd = _ssd(xbc, rowf, colf, state_l0_ssd.reshape(N_SAMPLE, 2, 1024, 128),
                      jnp.repeat(ssd_D, 64).reshape(1, D))
    o, new_dn = _dn(qkv, rowf, colf, state_l0_dn.reshape(N_SAMPLE, 2, 1024, 128))
    x = _l0_out(y, z, o, gate, x, mod0, mix_out_l0.astype(BF16), row(ssd_norm_w), row(dn_norm_w),
                row(norm_mix_post_l0))
    x = _ffn(x, mod0, row(norm_ffn_pre_l0), ffn_up_l0.astype(BF16), ffn_conv_w_l0, row(ffn_conv_b_l0),
             ffn_down_l0.astype(BF16), row(norm_ffn_post_l0))

    bs_full = jnp.repeat(sg_b_s.T, 256, axis=1)
    mlp, q, k, v = _l1_in(x, mod1, row(norm_mix_pre_l1), mix_in_l1.astype(BF16), row(sg_ln_w), row(sg_ln_b),
                          sg_w_s.astype(BF16), bs_full)
    sink_row = jnp.pad(attn_sink, (0, 120)).reshape(1, 128)
    attn = _ctx_attn(q, k, v, sink_row)
    cos2, sin2 = _rope_tables()
    attn = _lat_attn(q, k, v, cache_l1_k.reshape(N_SAMPLE, N_CTX, 256), cache_l1_v.reshape(N_SAMPLE, N_CTX, 256),
                     cos2, sin2, sink_row, attn)
    x = _l1_out(mlp, attn, x, mod1, mix_out_l1.astype(BF16), row(norm_mix_post_l1))
    x = _ffn(x, mod1, row(norm_ffn_pre_l1), ffn_up_l1.astype(BF16), ffn_conv_w_l1, row(ffn_conv_b_l1),
             ffn_down_l1.astype(BF16), row(norm_ffn_post_l1))

    n_p = N_PROMPT * SEG
    return (x[:n_p].reshape(N_PROMPT, SEG, D),
            x[n_p:].reshape(N_SAMPLE, 2048, D),
            new_ssd.reshape(N_PROMPT, 2, SSD_HEADS, 64, 128),
            new_dn.reshape(N_PROMPT, 2, DN_HEADS, 128, 128),
            k[:n_p].reshape(N_PROMPT, SEG, 2, 128),
            v[:n_p].reshape(N_PROMPT, SEG, 2, 128))
```

```python
import functools

import jax
import jax.numpy as jnp
from jax import lax
from jax.experimental import pallas as pl
from jax.experimental.pallas import tpu as pltpu

F32 = jnp.float32
BF16 = jnp.bfloat16

D = 1024
N_PROMPT = 32
SEG = 256
CHUNK = 128
SEG_PER_SAMPLE = 8
N_SAMPLE = 2
NSEG = N_PROMPT + N_SAMPLE * SEG_PER_SAMPLE
T = NSEG * SEG
HALO = 16
EPS = 1e-6
VMEM_LIMIT = 56 * 1024 * 1024

SSD_HEADS = 16
DN_HEADS = 8
D_FF = 2816
FF_CHUNK = 256

CH_DT = 0
CH_BETA = 48
CH_ACS = 64
CH_GCS = 96


def _dot(a, b):
    return jnp.dot(a, b, preferred_element_type=F32)


def _dot_nt(a, b):
    return lax.dot_general(a, b, (((1,), (1,)), ((), ())), preferred_element_type=F32)


def _dot_tn(a, b):
    return lax.dot_general(a, b, (((0,), (0,)), ((), ())), preferred_element_type=F32)


def _silu(x):
    return x * jax.nn.sigmoid(x)


def _rms(x, w):
    return x * lax.rsqrt(jnp.mean(x * x, axis=-1, keepdims=True) + EPS) * w


def _mod_index(i):
    return jnp.where(i < N_PROMPT, 0, 1 + jnp.maximum(i - N_PROMPT, 0) // SEG_PER_SAMPLE)


def _halo_flags(i):
    r = lax.rem(jnp.maximum(i - N_PROMPT, 0), SEG_PER_SAMPLE)
    in_sample = i >= N_PROMPT
    has_prev = jnp.where(jnp.logical_and(in_sample, r != 0), 1.0, 0.0)
    has_next = jnp.where(jnp.logical_and(in_sample, r != SEG_PER_SAMPLE - 1), 1.0, 0.0)
    return has_prev, has_next


def _row_spec(width):
    return pl.BlockSpec((SEG, width), lambda i: (i, 0))


def _prev_spec(first_seg=0, n_rows=T):
    return pl.BlockSpec((HALO, D), lambda i: (
        jnp.clip((i - first_seg) * (SEG // HALO) - 1, 0, n_rows // HALO - 1), 0))


def _next_spec(first_seg=0, n_rows=T):
    return pl.BlockSpec((HALO, D), lambda i: (
        jnp.clip((i - first_seg + 1) * (SEG // HALO), 0, n_rows // HALO - 1), 0))


def _prompt_spec(width):
    return pl.BlockSpec((SEG, width), lambda i: (jnp.minimum(i, N_PROMPT - 1), 0))


def _sample_spec(width):
    return pl.BlockSpec((SEG, width), lambda i: (jnp.maximum(i - N_PROMPT, 0), 0))


def _mod_spec():
    return pl.BlockSpec((1, 8, D), lambda i: (_mod_index(i), 0, 0))


def _const_spec(shape):
    nd = len(shape)
    return pl.BlockSpec(shape, lambda *_: (0,) * nd, pipeline_mode=pl.Buffered(1))


def _params(sem):
    return pltpu.CompilerParams(dimension_semantics=sem, vmem_limit_bytes=VMEM_LIMIT)


def _adaln_kernel(c_ref, w_ref, b_ref, o_ref):
    s = _silu(c_ref[...]).astype(BF16)
    o_ref[...] = _dot(s, w_ref[...].astype(BF16)) + b_ref[...]


def _adaln(cond8, w, b):
    tn = 768
    out = pl.pallas_call(
        _adaln_kernel,
        grid=(6 * D // tn,),
        in_specs=[pl.BlockSpec((8, D), lambda j: (0, 0)),
                  pl.BlockSpec((D, tn), lambda j: (0, j)),
                  pl.BlockSpec((1, tn), lambda j: (0, j))],
        out_specs=pl.BlockSpec((8, tn), lambda j: (0, j)),
        out_shape=jax.ShapeDtypeStruct((8, 6 * D), F32),
        compiler_params=_params(("arbitrary",)),
        name="adaln",
    )(cond8, w, b.reshape(1, -1))
    mod = out.reshape(8, 6, D)[:3]
    return jnp.pad(mod, ((0, 0), (0, 2), (0, 0)))


L0_CONV = 5120
L0_COLS = 7296
CONV_CHUNK = 512


def _pick_rows(i, prompt_ref, sample_ref):
    return jnp.where(i < N_PROMPT, prompt_ref[...], sample_ref[...])


def _halo_rows(i, x, xp_ref, xn_ref):
    xe = jnp.concatenate([x, xn_ref[...], xp_ref[...]], axis=0)
    has_prev, has_next = _halo_flags(i)
    r = lax.broadcasted_iota(jnp.int32, (2 * HALO, 1), 0)
    return xe, jnp.where(r < HALO, has_next, has_prev)


def _dwconv(p, valid, w_ref, b_ref, cs, width):
    n = p.shape[0]
    p = jnp.concatenate([p[0:SEG], p[SEG:n] * valid], axis=0)
    half = width // 2
    acc = p[0:SEG] * w_ref[half:half + 1, cs] + b_ref[:, cs]
    for k in range(width):
        if k != half:
            acc = acc + pltpu.roll(p, (half - k) % n, 0)[0:SEG] * w_ref[k:k + 1, cs]
    return acc


def _l0_in_kernel(xpr_ref, xsa_ref, xp_ref, xn_ref, m_ref, nw_ref, w_ref, cw_ref, cb_ref,
                  xbc_ref, qkv_ref, z_ref, gate_ref, small_ref):
    i = pl.program_id(0)
    shift = m_ref[0, 0:1, :]
    scale = m_ref[0, 1:2, :]
    xe, valid = _halo_rows(i, _pick_rows(i, xpr_ref, xsa_ref), xp_ref, xn_ref)
    he = (_rms(xe, nw_ref[...]) * (1.0 + scale) + shift).astype(BF16)
    for cc in range(L0_CONV // CONV_CHUNK):
        cs = slice(cc * CONV_CHUNK, (cc + 1) * CONV_CHUNK)
        acc = _dwconv(_dot(he, w_ref[:, cs]), valid, cw_ref, cb_ref, cs, 5)
        val = _silu(acc).astype(BF16)
        if cc < 4:
            xbc_ref[:, cs] = val
        else:
            for t in range(4):
                qkv_ref[(cc - 4) * 4 + t] = val[:, t * 128:(t + 1) * 128]
    hc = he[0:SEG]
    z_ref[...] = _dot(hc, w_ref[:, 5120:6144]).astype(BF16)
    gate_ref[...] = _dot(hc, w_ref[:, 6144:7168]).astype(BF16)
    small_ref[...] = _dot(hc, w_ref[:, 7168:7296])


def _l0_in(x_prompt, x_sample, modt, nw, w_main, cw, cb):
    n_s = N_SAMPLE * 2048
    return pl.pallas_call(
        _l0_in_kernel,
        grid=(NSEG,),
        in_specs=[_prompt_spec(D), _sample_spec(D),
                  _prev_spec(N_PROMPT, n_s), _next_spec(N_PROMPT, n_s), _mod_spec(),
                  _const_spec((1, D)), _const_spec((D, L0_COLS)),
                  _const_spec((5, L0_CONV)), _const_spec((1, L0_CONV))],
        out_specs=[_row_spec(2048),
                   pl.BlockSpec((24, SEG, 128), lambda i: (0, i, 0)),
                   _row_spec(D), _row_spec(D), _row_spec(128)],
        out_shape=[jax.ShapeDtypeStruct((T, 2048), BF16),
                   jax.ShapeDtypeStruct((24, T, 128), BF16),
                   jax.ShapeDtypeStruct((T, D), BF16),
                   jax.ShapeDtypeStruct((T, D), BF16),
                   jax.ShapeDtypeStruct((T, 128), F32)],
        compiler_params=_params(("arbitrary",)),
        name="l0_in",
    )(x_prompt, x_sample, x_sample, x_sample, modt, nw, w_main, cw, cb)


def _split3_dot(a, tri):
    a1 = a.astype(BF16)
    r1 = a - a1.astype(F32)
    a2 = r1.astype(BF16)
    a3 = (r1 - a2.astype(F32)).astype(BF16)
    return _dot(a1, tri) + _dot(a2, tri) + _dot(a3, tri)


def _prep_kernel(s_ref, bias_ref, mult_ref, rowf_ref, colf_ref):
    row = lax.broadcasted_iota(jnp.int32, (CHUNK, CHUNK), 0)
    col = lax.broadcasted_iota(jnp.int32, (CHUNK, CHUNK), 1)
    upper = jnp.where(row <= col, 1.0, 0.0).astype(BF16)
    lower = jnp.where(row >= col, 1.0, 0.0).astype(BF16)
    backward = ((row >= 16) & (row < 32)) | ((row >= 40) & (row < 48))
    for c in range(SEG // CHUNK):
        rs = slice(c * CHUNK, (c + 1) * CHUNK)
        raw_t = s_ref[rs, :].T
        xb = raw_t + bias_ref[...]
        sp = jnp.maximum(xb, 0.0) + jnp.log1p(jnp.exp(-jnp.abs(xb)))
        vals = jnp.where(row < 48, sp, jax.nn.sigmoid(raw_t))
        a = jnp.where(row < 48, sp * mult_ref[...], 0.0)
        cum = jnp.where(backward, _split3_dot(a, lower), _split3_dot(a, upper))
        out = jnp.concatenate([vals[0:64], cum[0:64]], axis=0)
        rowf_ref[c] = out
        colf_ref[rs, :] = out.T


def _prep(small, bias_col, mult_col):
    return pl.pallas_call(
        _prep_kernel,
        grid=(NSEG,),
        in_specs=[_row_spec(128), _const_spec((128, 1)), _const_spec((128, 1))],
        out_specs=[pl.BlockSpec((2, CHUNK, CHUNK), lambda i: (i, 0, 0)), _row_spec(128)],
        out_shape=[jax.ShapeDtypeStruct((T // CHUNK, CHUNK, CHUNK), F32),
                   jax.ShapeDtypeStruct((T, 128), F32)],
        compiler_params=_params(("arbitrary",)),
        name="scan_prep",
    )(small, bias_col, mult_col)


def _scan_seg(d, j):
    return j + d * (NSEG - 1 - 2 * j)


def _scan_batch(seg):
    return jnp.maximum(seg - N_PROMPT, 0) // SEG_PER_SAMPLE


def _scan_specs(heads, rows):
    seg_map = lambda d, j: _scan_seg(d, j)
    return dict(
        rowf=pl.BlockSpec((2, CHUNK, CHUNK), lambda d, j: (seg_map(d, j), 0, 0)),
        colf=pl.BlockSpec((SEG, 128), lambda d, j: (seg_map(d, j), 0)),
        h0=pl.BlockSpec((1, 1, heads, rows, 128), lambda d, j: (_scan_batch(seg_map(d, j)), d, 0, 0, 0)),
        ns=pl.BlockSpec((1, 1, heads, rows, 128),
                        lambda d, j: (jnp.minimum(seg_map(d, j), N_PROMPT - 1), d, 0, 0, 0)),
    )


def _scan_init(seg, direction, st, h0_ref):
    is_prompt = seg < N_PROMPT
    r = lax.rem(jnp.maximum(seg - N_PROMPT, 0), SEG_PER_SAMPLE)
    first = 0 if direction == 0 else SEG_PER_SAMPLE - 1

    @pl.when(is_prompt)
    def _():
        st[...] = jnp.zeros_like(st)

    @pl.when(jnp.logical_and(jnp.logical_not(is_prompt), r == first))
    def _():
        st[...] = h0_ref[0, 0].reshape(st.shape)

    return is_prompt


def _ssd_body(direction, seg, xbc_ref, rowf_ref, colf_ref, h0_ref, drow_ref, y_ref, ns_ref,
              st, m_s, e_s, w_s, gl_s):
    is_prompt = _scan_init(seg, direction, st, h0_ref)
    row = lax.broadcasted_iota(jnp.int32, (CHUNK, CHUNK), 0)
    col = lax.broadcasted_iota(jnp.int32, (CHUNK, CHUNK), 1)
    mask = (col <= row) if direction == 0 else (col >= row)
    lane = lax.broadcasted_iota(jnp.int32, (CHUNK, 256), 1)
    heads = [(lane >= 64 * hh) & (lane < 64 * hh + 64) for hh in range(4)]
    n_chunk = SEG // CHUNK
    order = range(n_chunk) if direction == 0 else range(n_chunk - 1, -1, -1)

    for c in range(n_chunk):
        rs = slice(c * CHUNK, (c + 1) * CHUNK)
        for g in range(4):
            bg = xbc_ref[rs, 1024 + 128 * g:1152 + 128 * g]
            cg = xbc_ref[rs, 1536 + 128 * g:1664 + 128 * g]
            gmat = _dot_nt(cg, bg)
            e_in = jnp.zeros((CHUNK, 256), F32)
            w_out = jnp.zeros((CHUNK, 256), F32)
            for hh in range(4):
                h = 4 * g + hh
                ch_dt = CH_DT + 16 * direction + h
                ch_ac = CH_ACS + 16 * direction + h
                ar = rowf_ref[c, ch_ac:ch_ac + 1, :]
                dt_r = rowf_ref[c, ch_dt:ch_dt + 1, :]
                last = ar[:, CHUNK - 1:CHUNK] if direction == 0 else ar[:, 0:1]
                ac_b = jnp.broadcast_to(colf_ref[rs, ch_ac:ch_ac + 1], (CHUNK, CHUNK))
                dt_b = jnp.broadcast_to(colf_ref[rs, ch_dt:ch_dt + 1], (CHUNK, CHUNK))
                lmat = jnp.exp(jnp.where(mask, ac_b - ar, -jnp.inf))
                m_s[16 * c + h] = (gmat * lmat * dt_r).astype(BF16)
                e_b = jnp.exp(ac_b)
                w_b = dt_b * jnp.exp(last - ac_b)
                e_in = jnp.where(heads[hh], jnp.concatenate([e_b, e_b], axis=1), e_in)
                w_out = jnp.where(heads[hh], jnp.concatenate([w_b, w_b], axis=1), w_out)
                gl_s[16 * c + h] = jnp.broadcast_to(jnp.exp(last), (8, CHUNK))
            e_s[4 * c + g] = e_in
            w_s[4 * c + g] = w_out

    for c in order:
        rs = slice(c * CHUNK, (c + 1) * CHUNK)
        for g in range(4):
            bg = xbc_ref[rs, 1024 + 128 * g:1152 + 128 * g]
            cg = xbc_ref[rs, 1536 + 128 * g:1664 + 128 * g]
            xf = xbc_ref[rs, 256 * g:256 * g + 256].astype(F32)
            y = _dot_nt(cg, st[256 * g:256 * g + 256, :].astype(BF16)) * e_s[4 * c + g]
            for hh in range(4):
                y = y + _dot(m_s[16 * c + 4 * g + hh], jnp.where(heads[hh], xf, 0.0).astype(BF16))
            if direction == 0:
                y = y + xf * drow_ref[:, 256 * g:256 * g + 256]
            y_ref[0, rs, 256 * g:256 * g + 256] = y.astype(BF16)
            upd = _dot_tn((xf * w_s[4 * c + g]).astype(BF16), bg)
            for hh in range(4):
                hr = slice(256 * g + 64 * hh, 256 * g + 64 * hh + 64)
                st[hr, :] = st[hr, :] * gl_s[16 * c + 4 * g + hh, 0:1, :] + upd[64 * hh:64 * hh + 64]

    @pl.when(is_prompt)
    def _():
        ns_ref[0, 0] = st[...].reshape(SSD_HEADS, 64, 128)


def _ssd_kernel(xbc_ref, rowf_ref, colf_ref, h0_ref, drow_ref, y_ref, ns_ref, *scratch):
    d = pl.program_id(0)
    seg = _scan_seg(d, pl.program_id(1))
    for direction in (0, 1):
        pl.when(d == direction)(functools.partial(
            _ssd_body, direction, seg, xbc_ref, rowf_ref, colf_ref, h0_ref, drow_ref, y_ref, ns_ref, *scratch))


def _ssd(xbc, rowf, colf, h0, drow):
    sp = _scan_specs(SSD_HEADS, 64)
    n_chunk = SEG // CHUNK
    return pl.pallas_call(
        _ssd_kernel,
        grid=(2, NSEG),
        in_specs=[pl.BlockSpec((SEG, 2048), lambda d, j: (_scan_seg(d, j), 0)),
                  sp["rowf"], sp["colf"], sp["h0"],
                  pl.BlockSpec((1, D), lambda d, j: (0, 0))],
        out_specs=[pl.BlockSpec((1, SEG, D), lambda d, j: (d, _scan_seg(d, j), 0)), sp["ns"]],
        out_shape=[jax.ShapeDtypeStruct((2, T, D), BF16),
                   jax.ShapeDtypeStruct((N_PROMPT, 2, SSD_HEADS, 64, 128), F32)],
        scratch_shapes=[pltpu.VMEM((1024, 128), F32),
                        pltpu.VMEM((n_chunk * SSD_HEADS, CHUNK, CHUNK), BF16),
                        pltpu.VMEM((n_chunk * 4, CHUNK, 256), F32),
                        pltpu.VMEM((n_chunk * 4, CHUNK, 256), F32),
                        pltpu.VMEM((n_chunk * SSD_HEADS, 8, CHUNK), F32)],
        compiler_params=_params(("arbitrary", "arbitrary")),
        name="ssd_scan",
    )(xbc, rowf, colf, h0, drow)


def _l2n(x):
    return x * lax.rsqrt(jnp.sum(x * x, axis=-1, keepdims=True) + EPS)


def _dn_body(direction, seg, qkv_ref, rowf_ref, colf_ref, s0_ref, lvl_ref, o_ref, ns_ref,
             st, a_s, d_s, rhs_s, qk_s, qd_s, ke_s, u_s, w_s, gl_s):
    is_prompt = _scan_init(seg, direction, st, s0_ref)
    row = lax.broadcasted_iota(jnp.int32, (CHUNK, CHUNK), 0)
    col = lax.broadcasted_iota(jnp.int32, (CHUNK, CHUNK), 1)
    tri = (col <= row) if direction == 0 else (col >= row)
    strict = (col < row) if direction == 0 else (col > row)
    eye = jnp.where(row == col, 1.0, 0.0)
    n_chunk = SEG // CHUNK
    order = range(n_chunk) if direction == 0 else range(n_chunk - 1, -1, -1)

    for h in range(DN_HEADS):
        ch_b = CH_BETA + 8 * direction + h
        ch_g = CH_GCS + 8 * direction + h
        for c in range(n_chunk):
            p = n_chunk * h + c
            rs = slice(c * CHUNK, (c + 1) * CHUNK)
            qn = _l2n(qkv_ref[h, rs, :].astype(F32)) * (128 ** -0.5)
            kn = _l2n(qkv_ref[DN_HEADS + h, rs, :].astype(F32))
            v = qkv_ref[2 * DN_HEADS + h, rs, :].astype(F32)
            beta = colf_ref[rs, ch_b:ch_b + 1]
            gc = colf_ref[rs, ch_g:ch_g + 1]
            gr = rowf_ref[c, ch_g:ch_g + 1, :]
            glast = gr[:, CHUNK - 1:CHUNK] if direction == 0 else gr[:, 0:1]
            decay = jnp.exp(jnp.where(tri, gc - gr, -jnp.inf))
            kb = kn * beta
            kq = _dot_nt(jnp.concatenate([kb, qn], axis=0).astype(BF16), kn.astype(BF16))
            a = jnp.where(strict, kq[0:CHUNK] * decay, 0.0)
            a_s[p] = a.astype(BF16)
            d_s[p] = eye - a * lvl_ref[0].astype(F32)
            qk_s[p] = (kq[CHUNK:2 * CHUNK] * decay).astype(BF16)
            eg = jnp.exp(gc)
            rhs_s[p] = jnp.concatenate([v * beta, kb * eg], axis=1).astype(BF16)
            qd_s[p] = (qn * eg).astype(BF16)
            ke_s[p] = (kn * jnp.exp(glast - gc)).astype(BF16)
            gl_s[p] = jnp.broadcast_to(jnp.exp(glast), (8, CHUNK))

    for lvl in range(1, 7):
        for p in range(n_chunk * DN_HEADS):
            tm = d_s[p]
            tb = tm.astype(BF16)
            a_off = a_s[p] * lvl_ref[lvl]
            d_s[p] = tm - _dot(tb, _dot(a_off, tb).astype(BF16))

    for p in range(n_chunk * DN_HEADS):
        uw = _dot(d_s[p].astype(BF16), rhs_s[p])
        u_s[p] = uw[:, 0:CHUNK]
        w_s[p] = uw[:, CHUNK:2 * CHUNK].astype(BF16)

    for h in range(DN_HEADS):
        s = st[h]
        for c in order:
            p = n_chunk * h + c
            sb = s.astype(BF16)
            wq = _dot(jnp.concatenate([w_s[p], qd_s[p]], axis=0), sb)
            vb = (u_s[p] - wq[0:CHUNK]).astype(BF16)
            o = wq[CHUNK:2 * CHUNK] + _dot(qk_s[p], vb)
            s = s * gl_s[p, 0:1, :] + _dot_tn(ke_s[p], vb)
            o_ref[0, h, c * CHUNK:(c + 1) * CHUNK, :] = o.astype(BF16)
        st[h] = s

    @pl.when(is_prompt)
    def _():
        ns_ref[0, 0] = st[...]


def _dn_kernel(qkv_ref, rowf_ref, colf_ref, s0_ref, lvl_ref, o_ref, ns_ref, *scratch):
    d = pl.program_id(0)
    seg = _scan_seg(d, pl.program_id(1))
    for direction in (0, 1):
        pl.when(d == direction)(functools.partial(
            _dn_body, direction, seg, qkv_ref, rowf_ref, colf_ref, s0_ref, lvl_ref, o_ref, ns_ref, *scratch))


def _sibling_masks():
    i = jnp.arange(CHUNK)[:, None]
    j = jnp.arange(CHUNK)[None, :]
    lv = [((i >> (l + 1)) == (j >> (l + 1))) & ((i >> l) != (j >> l)) for l in range(7)]
    return jnp.stack(lv).astype(BF16)


def _dn(qkv, rowf, colf, s0):
    sp = _scan_specs(DN_HEADS, CHUNK)
    n_prob = DN_HEADS * SEG // CHUNK
    mat = lambda dt: pltpu.VMEM((n_prob, CHUNK, CHUNK), dt)
    return pl.pallas_call(
        _dn_kernel,
        grid=(2, NSEG),
        in_specs=[pl.BlockSpec((24, SEG, 128), lambda d, j: (0, _scan_seg(d, j), 0)),
                  sp["rowf"], sp["colf"], sp["h0"], _const_spec((7, CHUNK, CHUNK))],
        out_specs=[pl.BlockSpec((1, DN_HEADS, SEG, 128), lambda d, j: (d, 0, _scan_seg(d, j), 0)),
                   sp["ns"]],
        out_shape=[jax.ShapeDtypeStruct((2, DN_HEADS, T, 128), BF16),
                   jax.ShapeDtypeStruct((N_PROMPT, 2, DN_HEADS, CHUNK, CHUNK), F32)],
        scratch_shapes=[pltpu.VMEM((DN_HEADS, CHUNK, CHUNK), F32),
                        mat(BF16), mat(F32),
                        pltpu.VMEM((n_prob, CHUNK, 2 * CHUNK), BF16),
                        mat(BF16), mat(BF16), mat(BF16),
                        mat(F32), mat(BF16),
                        pltpu.VMEM((n_prob, 8, CHUNK), F32)],
        compiler_params=_params(("arbitrary", "arbitrary")),
        name="dn_scan",
    )(qkv, rowf, colf, s0, _sibling_masks())


def _residual(x, m_ref, gate_row, out, pw_ref):
    gate = m_ref[0, gate_row:gate_row + 1, :]
    return x + gate * _rms(out, pw_ref[...])


def _l0_out_kernel(y_ref, z_ref, o_ref, gate_ref, xpr_ref, xsa_ref, m_ref, w_ref, snw_ref, dnw_ref, pw_ref,
                   xo_ref):
    y = (y_ref[0].astype(F32) + y_ref[1].astype(F32)) * _silu(z_ref[...].astype(F32))
    yn = _rms(y, snw_ref[...]).astype(BF16)
    heads = []
    for h in range(DN_HEADS):
        oh = o_ref[0, h].astype(F32) + o_ref[1, h].astype(F32)
        gh = gate_ref[:, 128 * h:128 * h + 128].astype(F32)
        heads.append((_rms(oh, dnw_ref[...]) * _silu(gh)).astype(BF16))
    on = jnp.concatenate(heads, axis=1)
    out = _dot(yn, w_ref[0:1024, :]) + _dot(on, w_ref[1024:2048, :])
    x = _pick_rows(pl.program_id(0), xpr_ref, xsa_ref)
    xo_ref[...] = _residual(x, m_ref, 2, out, pw_ref)


def _l0_out(y, z, o, gate, x_prompt, x_sample, modt, w_out, snw, dnw, pw):
    return pl.pallas_call(
        _l0_out_kernel,
        grid=(NSEG,),
        in_specs=[pl.BlockSpec((2, SEG, D), lambda i: (0, i, 0)), _row_spec(D),
                  pl.BlockSpec((2, DN_HEADS, SEG, 128), lambda i: (0, 0, i, 0)), _row_spec(D),
                  _prompt_spec(D), _sample_spec(D), _mod_spec(), _const_spec((2048, D)),
                  _const_spec((1, D)), _const_spec((1, 128)), _const_spec((1, D))],
        out_specs=_row_spec(D),
        out_shape=jax.ShapeDtypeStruct((T, D), F32),
        compiler_params=_params(("arbitrary",)),
        name="l0_out",
    )(y, z, o, gate, x_prompt, x_sample, modt, w_out, snw, dnw, pw)


def _l1_out_kernel(a_ref, bp_ref, bs_ref, x_ref, m_ref, w_ref, pw_ref, xo_ref):
    def finish(b_ref):
        out = _dot(a_ref[...], w_ref[0:1024, :]) + _dot(b_ref[...], w_ref[1024:2048, :])
        xo_ref[...] = _residual(x_ref[...], m_ref, 2, out, pw_ref)

    is_prompt = pl.program_id(0) < N_PROMPT
    pl.when(is_prompt)(functools.partial(finish, bp_ref))
    pl.when(jnp.logical_not(is_prompt))(functools.partial(finish, bs_ref))


def _l1_out(a, b_prompt, b_sample, x, modt, w_out, pw):
    return pl.pallas_call(
        _l1_out_kernel,
        grid=(NSEG,),
        in_specs=[_row_spec(D),
                  pl.BlockSpec((SEG, D), lambda i: (jnp.minimum(i, N_PROMPT - 1), 0)),
                  pl.BlockSpec((SEG, D), lambda i: (jnp.maximum(i - N_PROMPT, 0), 0)),
                  _row_spec(D), _mod_spec(), _const_spec((2048, D)), _const_spec((1, D))],
        out_specs=_row_spec(D),
        out_shape=jax.ShapeDtypeStruct((T, D), F32),
        compiler_params=_params(("arbitrary",)),
        name="l1_out",
    )(a, b_prompt, b_sample, x, modt, w_out, pw)


def _store_split(i, val, prompt_ref, sample_ref):
    @pl.when(i < N_PROMPT)
    def _():
        prompt_ref[...] = val

    @pl.when(i >= N_PROMPT)
    def _():
        sample_ref[...] = val


def _ffn_kernel(x_ref, xp_ref, xn_ref, m_ref, nw_ref, wu_ref, cw_ref, cb_ref, wd_ref, pw_ref,
                *out_refs):
    i = pl.program_id(0)
    shift = m_ref[0, 3:4, :]
    scale = m_ref[0, 4:5, :]
    xe, valid = _halo_rows(i, x_ref[...], xp_ref, xn_ref)
    he = (_rms(xe, nw_ref[...]) * (1.0 + scale) + shift).astype(BF16)
    acc = jnp.zeros((SEG, D), F32)
    for cc in range(D_FF // FF_CHUNK):
        halves = []
        for base in (0, D_FF):
            cs = slice(base + cc * FF_CHUNK, base + (cc + 1) * FF_CHUNK)
            halves.append(_dwconv(_dot(he, wu_ref[:, cs]), valid, cw_ref, cb_ref, cs, 3))
        g = (_silu(halves[0]) * halves[1]).astype(BF16)
        acc = acc + _dot(g, wd_ref[cc * FF_CHUNK:(cc + 1) * FF_CHUNK, :])
    xo = _residual(x_ref[...], m_ref, 5, acc, pw_ref)
    if len(out_refs) == 1:
        out_refs[0][...] = xo
    else:
        _store_split(i, xo, *out_refs)


def _ffn(x, modt, nw, wu, cw, cb, wd, pw, split_out=False):
    if split_out:
        out_specs = [_prompt_spec(D), _sample_spec(D)]
        out_shape = [jax.ShapeDtypeStruct((N_PROMPT * SEG, D), F32),
                     jax.ShapeDtypeStruct((N_SAMPLE * 2048, D), F32)]
    else:
        out_specs = _row_spec(D)
        out_shape = jax.ShapeDtypeStruct((T, D), F32)
    return pl.pallas_call(
        _ffn_kernel,
        grid=(NSEG,),
        in_specs=[_row_spec(D), _prev_spec(), _next_spec(), _mod_spec(), _const_spec((1, D)),
                  _const_spec((D, 2 * D_FF)), _const_spec((3, 2 * D_FF)), _const_spec((1, 2 * D_FF)),
                  _const_spec((D_FF, D)), _const_spec((1, D))],
        out_specs=out_specs,
        out_shape=out_shape,
        compiler_params=_params(("arbitrary",)),
        name="conv_ffn",
    )(x, x, x, modt, nw, wu, cw, cb, wd, pw)


def _l1_in_kernel(x_ref, m_ref, nw_ref, w_ref, lnw_ref, lnb_ref, ws_ref, bs_ref,
                  mlp_ref, q_ref, kp_ref, ks_ref, vp_ref, vs_ref):
    i = pl.program_id(0)
    shift = m_ref[0, 0:1, :]
    scale = m_ref[0, 1:2, :]
    h = (_rms(x_ref[...], nw_ref[...]) * (1.0 + scale) + shift).astype(BF16)
    u = jax.nn.gelu(_dot(h, w_ref[:, 0:1024]))
    gv = jax.nn.gelu(_dot(h, w_ref[:, 1024:2048]))
    gc = gv - jnp.mean(gv, axis=-1, keepdims=True)
    gn = gc * lax.rsqrt(jnp.mean(gc * gc, axis=-1, keepdims=True) + EPS) * lnw_ref[...] + lnb_ref[...]
    gb = gn.astype(BF16)
    for c in range(SEG // CHUNK):
        rs = slice(c * CHUNK, (c + 1) * CHUNK)
        for g in range(4):
            cs = slice(256 * g, 256 * g + 256)
            sv = _dot(ws_ref[g], gb[rs, cs]) + bs_ref[:, cs]
            mlp_ref[rs, cs] = (u[rs, cs] * sv).astype(BF16)
    q_ref[...] = _dot(h, w_ref[:, 2048:3072]).astype(BF16)
    _store_split(i, _dot(h, w_ref[:, 3072:3328]), kp_ref, ks_ref)
    _store_split(i, _dot(h, w_ref[:, 3328:3584]), vp_ref, vs_ref)


def _l1_in(x, modt, nw, w_in, lnw, lnb, ws, bs):
    n_p, n_s = N_PROMPT * SEG, N_SAMPLE * 2048
    return pl.pallas_call(
        _l1_in_kernel,
        grid=(NSEG,),
        in_specs=[_row_spec(D), _mod_spec(), _const_spec((1, D)), _const_spec((D, 3584)),
                  _const_spec((1, D)), _const_spec((1, D)), _const_spec((4, CHUNK, CHUNK)),
                  _const_spec((CHUNK, D))],
        out_specs=[_row_spec(D), _row_spec(D), _prompt_spec(256), _sample_spec(256),
                   _prompt_spec(256), _sample_spec(256)],
        out_shape=[jax.ShapeDtypeStruct((T, D), BF16), jax.ShapeDtypeStruct((T, D), BF16),
                   jax.ShapeDtypeStruct((n_p, 256), F32), jax.ShapeDtypeStruct((n_s, 256), F32),
                   jax.ShapeDtypeStruct((n_p, 256), F32), jax.ShapeDtypeStruct((n_s, 256), F32)],
        compiler_params=_params(("arbitrary",)),
        name="l1_in",
    )(x, modt, nw, w_in, lnw, lnb, ws, bs)


def _sink_softmax_pv(s, sink_col, vb):
    m = jnp.maximum(jnp.max(s, axis=-1, keepdims=True), sink_col)
    p = jnp.exp(s - m)
    den = jnp.sum(p, axis=-1, keepdims=True) + jnp.exp(sink_col - m)
    return _dot(p.astype(BF16), vb) / den


def _sink_column(sink_ref, kh, rows):
    parts = [jnp.broadcast_to(sink_ref[:, 4 * kh + g:4 * kh + g + 1], (rows, 1)) for g in range(4)]
    return jnp.concatenate(parts, axis=0)


def _ctx_attn_kernel(q_ref, k_ref, v_ref, sink_ref, o_ref):
    scale = 128 ** -0.5
    for kh in range(2):
        kb = k_ref[:, 128 * kh:128 * kh + 128].astype(BF16)
        vb = v_ref[:, 128 * kh:128 * kh + 128].astype(BF16)
        q4 = jnp.concatenate([q_ref[:, 128 * (4 * kh + g):128 * (4 * kh + g) + 128] for g in range(4)], axis=0)
        s = _dot_nt(q4, kb) * scale
        o = _sink_softmax_pv(s, _sink_column(sink_ref, kh, SEG), vb)
        for g in range(4):
            hd = 4 * kh + g
            o_ref[:, 128 * hd:128 * hd + 128] = o[g * SEG:(g + 1) * SEG].astype(BF16)


def _ctx_attn(q, k, v, sink_row):
    return pl.pallas_call(
        _ctx_attn_kernel,
        grid=(N_PROMPT,),
        in_specs=[_row_spec(D), _row_spec(256), _row_spec(256), _const_spec((1, 128))],
        out_specs=_row_spec(D),
        out_shape=jax.ShapeDtypeStruct((N_PROMPT * SEG, D), BF16),
        compiler_params=_params(("arbitrary",)),
        name="ctx_attn",
    )(q, k, v, sink_row)


N_QBLK = 2048 // CHUNK
N_CTX = 512


def _rope(x, cos2, sin2):
    return x * cos2 + pltpu.roll(x, 64, 1) * sin2


def _lat_attn_kernel(q_ref, kp_ref, kc_ref, kn_ref, vp_ref, vc_ref, vn_ref, ck_ref, cv_ref,
                     cosq_ref, sinq_ref, cosp_ref, sinp_ref, cosn_ref, sinn_ref, sink_ref, o_ref):
    qi = pl.program_id(1)
    scale = 128 ** -0.5
    nk = 3 * CHUNK + N_CTX
    r = lax.broadcasted_iota(jnp.int32, (4 * CHUNK, nk), 0) & (CHUNK - 1)
    c = lax.broadcasted_iota(jnp.int32, (4 * CHUNK, nk), 1)
    far = 4 * CHUNK
    is_prev = c < CHUNK
    is_next = (c >= 2 * CHUNK) & (c < 3 * CHUNK)
    ok_prev = is_prev & (c >= r + jnp.where(qi > 0, 0, far))
    ok_next = is_next & (c - 2 * CHUNK <= r - jnp.where(qi < N_QBLK - 1, 0, far))
    mask = ok_prev | ok_next | jnp.logical_not(is_prev | is_next)
    for kh in range(2):
        ks = slice(128 * kh, 128 * kh + 128)
        k_all = jnp.concatenate([
            _rope(kp_ref[:, ks], cosp_ref[...], sinp_ref[...]).astype(BF16),
            _rope(kc_ref[:, ks], cosq_ref[...], sinq_ref[...]).astype(BF16),
            _rope(kn_ref[:, ks], cosn_ref[...], sinn_ref[...]).astype(BF16),
            ck_ref[0, :, ks].astype(BF16)], axis=0)
        v_all = jnp.concatenate([vp_ref[:, ks], vc_ref[:, ks], vn_ref[:, ks], cv_ref[0, :, ks]],
                                axis=0).astype(BF16)
        q4 = jnp.concatenate([
            (_rope(q_ref[:, 128 * (4 * kh + g):128 * (4 * kh + g) + 128].astype(F32),
                   cosq_ref[...], sinq_ref[...]) * scale).astype(BF16) for g in range(4)], axis=0)
        s = jnp.where(mask, _dot_nt(q4, k_all), -jnp.inf)
        o = _sink_softmax_pv(s, _sink_column(sink_ref, kh, CHUNK), v_all)
        for g in range(4):
            hd = 4 * kh + g
            o_ref[:, 128 * hd:128 * hd + 128] = o[g * CHUNK:(g + 1) * CHUNK].astype(BF16)


def _lat_attn(q, k, v, cache_k, cache_v, cos2, sin2, sink_row):
    base = N_PROMPT * SEG // CHUNK

    def blk(b, qi):
        return b * N_QBLK + qi

    def prev_blk(b, qi):
        return b * N_QBLK + jnp.maximum(qi - 1, 0)

    def next_blk(b, qi):
        return b * N_QBLK + jnp.minimum(qi + 1, N_QBLK - 1)

    kv_spec = lambda f: pl.BlockSpec((CHUNK, 256), lambda b, qi: (f(b, qi), 0))
    tab = lambda f: pl.BlockSpec((CHUNK, 128), lambda b, qi: (f(qi), 0))
    cache_spec = pl.BlockSpec((1, N_CTX, 256), lambda b, qi: (b, 0, 0))
    same = lambda qi: qi
    before = lambda qi: jnp.maximum(qi - 1, 0)
    after = lambda qi: jnp.minimum(qi + 1, N_QBLK - 1)
    return pl.pallas_call(
        _lat_attn_kernel,
        grid=(N_SAMPLE, N_QBLK),
        in_specs=[pl.BlockSpec((CHUNK, D), lambda b, qi: (base + blk(b, qi), 0)),
                  kv_spec(prev_blk), kv_spec(blk), kv_spec(next_blk),
                  kv_spec(prev_blk), kv_spec(blk), kv_spec(next_blk),
                  cache_spec, cache_spec,
                  tab(same), tab(same), tab(before), tab(before), tab(after), tab(after),
                  pl.BlockSpec((1, 128), lambda b, qi: (0, 0))],
        out_specs=pl.BlockSpec((CHUNK, D), lambda b, qi: (blk(b, qi), 0)),
        out_shape=jax.ShapeDtypeStruct((N_SAMPLE * 2048, D), BF16),
        compiler_params=_params(("arbitrary", "arbitrary")),
        name="lat_attn",
    )(q, k, k, k, v, v, v, cache_k, cache_v, cos2, sin2, cos2, sin2, cos2, sin2, sink_row)


def _rope_tables():
    length = 2048
    rows = length // 64
    rowp = jnp.repeat(jnp.arange(rows, dtype=F32), 64)
    colp = jnp.tile(jnp.arange(64, dtype=F32), rows)
    inv = 10000.0 ** (-jnp.arange(32, dtype=F32) / 32)
    ang = jnp.concatenate([rowp[:, None] * inv, colp[:, None] * inv], axis=-1)
    cos, sin = jnp.cos(ang), jnp.sin(ang)
    return jnp.concatenate([cos, cos], axis=-1), jnp.concatenate([-sin, sin], axis=-1)


def kernel(x_prompt, x_sample, state_l0_ssd, state_l0_dn, cache_l1_k, cache_l1_v, c, c_ctx, mod_w_l0, mod_b_l0, norm_mix_pre_l0, norm_mix_post_l0, norm_ffn_pre_l0, norm_ffn_post_l0, ffn_up_l0, ffn_conv_w_l0, ffn_conv_b_l0, ffn_down_l0, mod_w_l1, mod_b_l1, norm_mix_pre_l1, norm_mix_post_l1, norm_ffn_pre_l1, norm_ffn_post_l1, ffn_up_l1, ffn_conv_w_l1, ffn_conv_b_l1, ffn_down_l1, mix_in_l0, mix_out_l0, ssd_conv_w, ssd_conv_b, ssd_dt_bias, ssd_A_log, ssd_D, ssd_norm_w, dn_conv_w, dn_dt_bias, dn_A_log, dn_norm_w, mix_in_l1, mix_out_l1, sg_ln_w, sg_ln_b, sg_w_s, sg_b_s, attn_sink):
    row = lambda a: a.reshape(1, -1).astype(F32)
    x_p = x_prompt.reshape(N_PROMPT * SEG, D)
    x_s = x_sample.reshape(N_SAMPLE * 2048, D)
    cond8 = jnp.concatenate([c_ctx[None, :], c, jnp.zeros((5, D), F32)], axis=0)
    mod0 = _adaln(cond8, mod_w_l0, mod_b_l0)
    mod1 = _adaln(cond8, mod_w_l1, mod_b_l1)

    w = mix_in_l0
    w_main = jnp.concatenate([w[:, 1024:3072], w[:, 3104:6176], w[:, 0:1024], w[:, 6176:7200],
                              w[:, 3072:3104], w[:, 7200:7232], jnp.zeros((D, 64), F32)], axis=1).astype(BF16)
    cw = jnp.concatenate([ssd_conv_w, dn_conv_w], axis=1)
    cb = jnp.concatenate([ssd_conv_b, jnp.zeros((3072,), F32)]).reshape(1, -1)
    xbc, qkv, z, gate, small = _l0_in(x_p, x_s, mod0, row(norm_mix_pre_l0), w_main, cw, cb)
    bias_col = jnp.concatenate([ssd_dt_bias.reshape(-1), dn_dt_bias.reshape(-1), jnp.zeros((80,), F32)]).reshape(128, 1)
    mult_col = jnp.concatenate([-jnp.exp(ssd_A_log.reshape(-1)), -jnp.exp(dn_A_log.reshape(-1)),
                                jnp.zeros((80,), F32)]).reshape(128, 1)
    rowf, colf = _prep(small, bias_col, mult_col)
    y, new_ssd = _ssd(xbc, rowf, colf, state_l0_ssd, jnp.repeat(ssd_D, 64).reshape(1, D))
    o, new_dn = _dn(qkv, rowf, colf, state_l0_dn)
    x = _l0_out(y, z, o, gate, x_p, x_s, mod0, mix_out_l0.astype(BF16), row(ssd_norm_w), row(dn_norm_w),
                row(norm_mix_post_l0))
    x = _ffn(x, mod0, row(norm_ffn_pre_l0), ffn_up_l0.astype(BF16), ffn_conv_w_l0, row(ffn_conv_b_l0),
             ffn_down_l0.astype(BF16), row(norm_ffn_post_l0))

    bs_full = jnp.repeat(sg_b_s.T, 256, axis=1)
    mlp, q, k_p, k_s, v_p, v_s = _l1_in(x, mod1, row(norm_mix_pre_l1), mix_in_l1.astype(BF16), row(sg_ln_w),
                                        row(sg_ln_b), sg_w_s.astype(BF16), bs_full)
    sink_row = jnp.pad(attn_sink, (0, 120)).reshape(1, 128)
    attn_p = _ctx_attn(q, k_p, v_p, sink_row)
    cos2, sin2 = _rope_tables()
    attn_s = _lat_attn(q, k_s, v_s, cache_l1_k.reshape(N_SAMPLE, N_CTX, 256),
                       cache_l1_v.reshape(N_SAMPLE, N_CTX, 256), cos2, sin2, sink_row)
    x = _l1_out(mlp, attn_p, attn_s, x, mod1, mix_out_l1.astype(BF16), row(norm_mix_post_l1))
    y_p, y_s = _ffn(x, mod1, row(norm_ffn_pre_l1), ffn_up_l1.astype(BF16), ffn_conv_w_l1, row(ffn_conv_b_l1),
                    ffn_down_l1.astype(BF16), row(norm_ffn_post_l1), split_out=True)

    return (y_p.reshape(N_PROMPT, SEG, D),
            y_s.reshape(N_SAMPLE, 2048, D),
            new_ssd,
            new_dn,
            k_p.reshape(N_PROMPT, SEG, 2, 128),
            v_p.reshape(N_PROMPT, SEG, 2, 128))
```

```python
import functools

import jax
import jax.numpy as jnp
from jax import lax
from jax.experimental import pallas as pl
from jax.experimental.pallas import tpu as pltpu

F32 = jnp.float32
BF16 = jnp.bfloat16

D = 1024
N_PROMPT = 32
SEG = 256
CHUNK = 128
SEG_PER_SAMPLE = 8
N_SAMPLE = 2
NSEG = N_PROMPT + N_SAMPLE * SEG_PER_SAMPLE
T = NSEG * SEG
HALO = 8
EPS = 1e-6
VMEM_LIMIT = 56 * 1024 * 1024

SSD_HEADS = 16
DN_HEADS = 8
D_FF = 2816
FF_CHUNK = 256

CH_DT = 0
CH_BETA = 48
CH_ACS = 64
CH_GCS = 96


def _dot(a, b):
    return jnp.dot(a, b, preferred_element_type=F32)


def _dot_nt(a, b):
    return lax.dot_general(a, b, (((1,), (1,)), ((), ())), preferred_element_type=F32)


def _dot_tn(a, b):
    return lax.dot_general(a, b, (((0,), (0,)), ((), ())), preferred_element_type=F32)


def _silu(x):
    return x * jax.nn.sigmoid(x)


def _rms(x, w):
    return x * lax.rsqrt(jnp.mean(x * x, axis=-1, keepdims=True) + EPS) * w


def _mod_index(i):
    return jnp.where(i < N_PROMPT, 0, 1 + jnp.maximum(i - N_PROMPT, 0) // SEG_PER_SAMPLE)


def _halo_flags(i):
    r = lax.rem(jnp.maximum(i - N_PROMPT, 0), SEG_PER_SAMPLE)
    in_sample = i >= N_PROMPT
    has_prev = jnp.where(jnp.logical_and(in_sample, r != 0), 1.0, 0.0)
    has_next = jnp.where(jnp.logical_and(in_sample, r != SEG_PER_SAMPLE - 1), 1.0, 0.0)
    return has_prev, has_next


def _row_spec(width):
    return pl.BlockSpec((SEG, width), lambda i: (i, 0))


def _prev_spec(first_seg=0, n_rows=T):
    return pl.BlockSpec((HALO, D), lambda i: (
        jnp.clip((i - first_seg) * (SEG // HALO) - 1, 0, n_rows // HALO - 1), 0))


def _next_spec(first_seg=0, n_rows=T):
    return pl.BlockSpec((HALO, D), lambda i: (
        jnp.clip((i - first_seg + 1) * (SEG // HALO), 0, n_rows // HALO - 1), 0))


def _prompt_spec(width):
    return pl.BlockSpec((SEG, width), lambda i: (jnp.minimum(i, N_PROMPT - 1), 0))


def _sample_spec(width):
    return pl.BlockSpec((SEG, width), lambda i: (jnp.maximum(i - N_PROMPT, 0), 0))


def _mod_spec():
    return pl.BlockSpec((1, 8, D), lambda i: (_mod_index(i), 0, 0))


def _const_spec(shape):
    nd = len(shape)
    return pl.BlockSpec(shape, lambda *_: (0,) * nd, pipeline_mode=pl.Buffered(1))


def _params(sem):
    return pltpu.CompilerParams(dimension_semantics=sem, vmem_limit_bytes=VMEM_LIMIT)


def _adaln_kernel(c_ref, w_ref, b_ref, o_ref):
    s = _silu(c_ref[...]).astype(BF16)
    o_ref[...] = _dot(s, w_ref[...].astype(BF16)) + b_ref[...]


def _adaln(cond8, w, b):
    tn = 768
    out = pl.pallas_call(
        _adaln_kernel,
        grid=(6 * D // tn,),
        in_specs=[pl.BlockSpec((8, D), lambda j: (0, 0)),
                  pl.BlockSpec((D, tn), lambda j: (0, j)),
                  pl.BlockSpec((1, tn), lambda j: (0, j))],
        out_specs=pl.BlockSpec((8, tn), lambda j: (0, j)),
        out_shape=jax.ShapeDtypeStruct((8, 6 * D), F32),
        compiler_params=_params(("arbitrary",)),
        name="adaln",
    )(cond8, w, b.reshape(1, -1))
    mod = out.reshape(8, 6, D)[:3]
    return jnp.pad(mod, ((0, 0), (0, 2), (0, 0)))


L0_CONV = 5120
L0_COLS = 7296
CONV_CHUNK = 512


def _pick_rows(i, prompt_ref, sample_ref):
    return jnp.where(i < N_PROMPT, prompt_ref[...], sample_ref[...])


def _halo_rows(i, x, xp_ref, xn_ref):
    xe = jnp.concatenate([x, xn_ref[...], xp_ref[...]], axis=0)
    has_prev, has_next = _halo_flags(i)
    r = lax.broadcasted_iota(jnp.int32, (2 * HALO, 1), 0)
    return xe, jnp.where(r < HALO, has_next, has_prev)


def _dwconv(p, valid, w_ref, b_ref, cs, width):
    n = p.shape[0]
    p = jnp.concatenate([p[0:SEG], p[SEG:n] * valid], axis=0)
    half = width // 2
    acc = p[0:SEG] * w_ref[half:half + 1, cs] + b_ref[:, cs]
    for k in range(width):
        if k != half:
            acc = acc + pltpu.roll(p, (half - k) % n, 0)[0:SEG] * w_ref[k:k + 1, cs]
    return acc


def _l0_in_kernel(xpr_ref, xsa_ref, xp_ref, xn_ref, m_ref, nw_ref, w_ref, cw_ref, cb_ref,
                  xbc_ref, qkv_ref, z_ref, gate_ref, small_ref):
    i = pl.program_id(0)
    shift = m_ref[0, 0:1, :]
    scale = m_ref[0, 1:2, :]
    xe, valid = _halo_rows(i, _pick_rows(i, xpr_ref, xsa_ref), xp_ref, xn_ref)
    he = (_rms(xe, nw_ref[...]) * (1.0 + scale) + shift).astype(BF16)
    hc = he[0:SEG]

    def plain(j):
        if j < 4:
            z_ref[:, 256 * j:256 * j + 256] = _dot(hc, w_ref[:, 5120 + 256 * j:5376 + 256 * j]).astype(BF16)
        elif j < 8:
            j -= 4
            gate_ref[:, 256 * j:256 * j + 256] = _dot(hc, w_ref[:, 6144 + 256 * j:6400 + 256 * j]).astype(BF16)
        elif j == 8:
            small_ref[...] = _dot(hc, w_ref[:, 7168:7296])

    n_conv = L0_CONV // CONV_CHUNK
    conv_cols = lambda cc: slice(cc * CONV_CHUNK, (cc + 1) * CONV_CHUNK)
    nxt = _dot(he, w_ref[:, conv_cols(0)])
    for cc in range(n_conv):
        cur = nxt
        if cc + 1 < n_conv:
            nxt = _dot(he, w_ref[:, conv_cols(cc + 1)])
        plain(cc)
        cs = conv_cols(cc)
        val = _silu(_dwconv(cur, valid, cw_ref, cb_ref, cs, 5)).astype(BF16)
        if cc < 4:
            xbc_ref[:, cs] = val
        else:
            for t in range(4):
                qkv_ref[(cc - 4) * 4 + t] = val[:, t * 128:(t + 1) * 128]


def _l0_in(x_prompt, x_sample, modt, nw, w_main, cw, cb):
    n_s = N_SAMPLE * 2048
    return pl.pallas_call(
        _l0_in_kernel,
        grid=(NSEG,),
        in_specs=[_prompt_spec(D), _sample_spec(D),
                  _prev_spec(N_PROMPT, n_s), _next_spec(N_PROMPT, n_s), _mod_spec(),
                  _const_spec((1, D)), _const_spec((D, L0_COLS)),
                  _const_spec((5, L0_CONV)), _const_spec((1, L0_CONV))],
        out_specs=[_row_spec(2048),
                   pl.BlockSpec((24, SEG, 128), lambda i: (0, i, 0)),
                   _row_spec(D), _row_spec(D), _row_spec(128)],
        out_shape=[jax.ShapeDtypeStruct((T, 2048), BF16),
                   jax.ShapeDtypeStruct((24, T, 128), BF16),
                   jax.ShapeDtypeStruct((T, D), BF16),
                   jax.ShapeDtypeStruct((T, D), BF16),
                   jax.ShapeDtypeStruct((T, 128), F32)],
        compiler_params=_params(("arbitrary",)),
        name="l0_in",
    )(x_prompt, x_sample, x_sample, x_sample, modt, nw, w_main, cw, cb)


def _split3_dot(a, tri):
    a1 = a.astype(BF16)
    r1 = a - a1.astype(F32)
    a2 = r1.astype(BF16)
    a3 = (r1 - a2.astype(F32)).astype(BF16)
    return _dot(a1, tri) + _dot(a2, tri) + _dot(a3, tri)


def _prep_kernel(s_ref, bias_ref, mult_ref, rowf_ref, colf_ref):
    row = lax.broadcasted_iota(jnp.int32, (CHUNK, CHUNK), 0)
    col = lax.broadcasted_iota(jnp.int32, (CHUNK, CHUNK), 1)
    upper = jnp.where(row <= col, 1.0, 0.0).astype(BF16)
    lower = jnp.where(row >= col, 1.0, 0.0).astype(BF16)
    backward = ((row >= 16) & (row < 32)) | ((row >= 40) & (row < 48))
    for c in range(SEG // CHUNK):
        rs = slice(c * CHUNK, (c + 1) * CHUNK)
        raw_t = s_ref[rs, :].T
        xb = raw_t + bias_ref[...]
        sp = jnp.maximum(xb, 0.0) + jnp.log1p(jnp.exp(-jnp.abs(xb)))
        vals = jnp.where(row < 48, sp, jax.nn.sigmoid(raw_t))
        a = jnp.where(row < 48, sp * mult_ref[...], 0.0)
        cum = jnp.where(backward, _split3_dot(a, lower), _split3_dot(a, upper))
        out = jnp.concatenate([vals[0:64], cum[0:64]], axis=0)
        rowf_ref[c] = out
        colf_ref[rs, :] = out.T


def _prep(small, bias_col, mult_col):
    return pl.pallas_call(
        _prep_kernel,
        grid=(NSEG,),
        in_specs=[_row_spec(128), _const_spec((128, 1)), _const_spec((128, 1))],
        out_specs=[pl.BlockSpec((2, CHUNK, CHUNK), lambda i: (i, 0, 0)), _row_spec(128)],
        out_shape=[jax.ShapeDtypeStruct((T // CHUNK, CHUNK, CHUNK), F32),
                   jax.ShapeDtypeStruct((T, 128), F32)],
        compiler_params=_params(("arbitrary",)),
        name="scan_prep",
    )(small, bias_col, mult_col)


def _scan_seg(d, j):
    return j + d * (NSEG - 1 - 2 * j)


def _scan_batch(seg):
    return jnp.maximum(seg - N_PROMPT, 0) // SEG_PER_SAMPLE


def _scan_specs(heads, rows):
    seg_map = lambda d, j: _scan_seg(d, j)
    return dict(
        rowf=pl.BlockSpec((2, CHUNK, CHUNK), lambda d, j: (seg_map(d, j), 0, 0)),
        colf=pl.BlockSpec((SEG, 128), lambda d, j: (seg_map(d, j), 0)),
        h0=pl.BlockSpec((1, 1, heads, rows, 128), lambda d, j: (_scan_batch(seg_map(d, j)), d, 0, 0, 0)),
        ns=pl.BlockSpec((1, 1, heads, rows, 128),
                        lambda d, j: (jnp.minimum(seg_map(d, j), N_PROMPT - 1), d, 0, 0, 0)),
    )


def _scan_init(seg, direction, st, h0_ref):
    is_prompt = seg < N_PROMPT
    r = lax.rem(jnp.maximum(seg - N_PROMPT, 0), SEG_PER_SAMPLE)
    first = 0 if direction == 0 else SEG_PER_SAMPLE - 1

    @pl.when(is_prompt)
    def _():
        st[...] = jnp.zeros_like(st)

    @pl.when(jnp.logical_and(jnp.logical_not(is_prompt), r == first))
    def _():
        st[...] = h0_ref[0, 0].reshape(st.shape)

    return is_prompt


def _ssd_body(direction, seg, xbc_ref, rowf_ref, colf_ref, h0_ref, drow_ref, y_ref, ns_ref,
              st, m_s, e_s, w_s, gl_s):
    is_prompt = _scan_init(seg, direction, st, h0_ref)
    row = lax.broadcasted_iota(jnp.int32, (CHUNK, CHUNK), 0)
    col = lax.broadcasted_iota(jnp.int32, (CHUNK, CHUNK), 1)
    mask = (col <= row) if direction == 0 else (col >= row)
    lane = lax.broadcasted_iota(jnp.int32, (CHUNK, 256), 1)
    heads = [(lane >= 64 * hh) & (lane < 64 * hh + 64) for hh in range(4)]
    n_chunk = SEG // CHUNK
    order = range(n_chunk) if direction == 0 else range(n_chunk - 1, -1, -1)

    for c in range(n_chunk):
        rs = slice(c * CHUNK, (c + 1) * CHUNK)
        for g in range(4):
            bg = xbc_ref[rs, 1024 + 128 * g:1152 + 128 * g]
            cg = xbc_ref[rs, 1536 + 128 * g:1664 + 128 * g]
            gmat = _dot_nt(cg, bg)
            e_in = jnp.zeros((CHUNK, 256), F32)
            w_out = jnp.zeros((CHUNK, 256), F32)
            for hh in range(4):
                h = 4 * g + hh
                ch_dt = CH_DT + 16 * direction + h
                ch_ac = CH_ACS + 16 * direction + h
                ar = rowf_ref[c, ch_ac:ch_ac + 1, :]
                dt_r = rowf_ref[c, ch_dt:ch_dt + 1, :]
                last = ar[:, CHUNK - 1:CHUNK] if direction == 0 else ar[:, 0:1]
                ac_b = jnp.broadcast_to(colf_ref[rs, ch_ac:ch_ac + 1], (CHUNK, CHUNK))
                dt_b = jnp.broadcast_to(colf_ref[rs, ch_dt:ch_dt + 1], (CHUNK, CHUNK))
                lmat = jnp.exp(jnp.where(mask, ac_b - ar, -jnp.inf))
                m_s[16 * c + h] = (gmat * lmat * dt_r).astype(BF16)
                e_b = jnp.exp(ac_b)
                w_b = dt_b * jnp.exp(last - ac_b)
                e_in = jnp.where(heads[hh], jnp.concatenate([e_b, e_b], axis=1), e_in)
                w_out = jnp.where(heads[hh], jnp.concatenate([w_b, w_b], axis=1), w_out)
                gl_s[16 * c + h] = jnp.broadcast_to(jnp.exp(last), (8, CHUNK))
            e_s[4 * c + g] = e_in
            w_s[4 * c + g] = w_out

    for c in order:
        rs = slice(c * CHUNK, (c + 1) * CHUNK)
        for g in range(4):
            bg = xbc_ref[rs, 1024 + 128 * g:1152 + 128 * g]
            cg = xbc_ref[rs, 1536 + 128 * g:1664 + 128 * g]
            xf = xbc_ref[rs, 256 * g:256 * g + 256].astype(F32)
            y = _dot_nt(cg, st[256 * g:256 * g + 256, :].astype(BF16)) * e_s[4 * c + g]
            for hh in range(4):
                y = y + _dot(m_s[16 * c + 4 * g + hh], jnp.where(heads[hh], xf, 0.0).astype(BF16))
            if direction == 0:
                y = y + xf * drow_ref[:, 256 * g:256 * g + 256]
            y_ref[0, rs, 256 * g:256 * g + 256] = y.astype(BF16)
            upd = _dot_tn((xf * w_s[4 * c + g]).astype(BF16), bg)
            for hh in range(4):
                hr = slice(256 * g + 64 * hh, 256 * g + 64 * hh + 64)
                st[hr, :] = st[hr, :] * gl_s[16 * c + 4 * g + hh, 0:1, :] + upd[64 * hh:64 * hh + 64]

    @pl.when(is_prompt)
    def _():
        ns_ref[0, 0] = st[...].reshape(SSD_HEADS, 64, 128)


def _ssd_kernel(xbc_ref, rowf_ref, colf_ref, h0_ref, drow_ref, y_ref, ns_ref, *scratch):
    d = pl.program_id(0)
    seg = _scan_seg(d, pl.program_id(1))
    for direction in (0, 1):
        pl.when(d == direction)(functools.partial(
            _ssd_body, direction, seg, xbc_ref, rowf_ref, colf_ref, h0_ref, drow_ref, y_ref, ns_ref, *scratch))


def _ssd(xbc, rowf, colf, h0, drow):
    sp = _scan_specs(SSD_HEADS, 64)
    n_chunk = SEG // CHUNK
    return pl.pallas_call(
        _ssd_kernel,
        grid=(2, NSEG),
        in_specs=[pl.BlockSpec((SEG, 2048), lambda d, j: (_scan_seg(d, j), 0)),
                  sp["rowf"], sp["colf"], sp["h0"],
                  pl.BlockSpec((1, D), lambda d, j: (0, 0))],
        out_specs=[pl.BlockSpec((1, SEG, D), lambda d, j: (d, _scan_seg(d, j), 0)), sp["ns"]],
        out_shape=[jax.ShapeDtypeStruct((2, T, D), BF16),
                   jax.ShapeDtypeStruct((N_PROMPT, 2, SSD_HEADS, 64, 128), F32)],
        scratch_shapes=[pltpu.VMEM((1024, 128), F32),
                        pltpu.VMEM((n_chunk * SSD_HEADS, CHUNK, CHUNK), BF16),
                        pltpu.VMEM((n_chunk * 4, CHUNK, 256), F32),
                        pltpu.VMEM((n_chunk * 4, CHUNK, 256), F32),
                        pltpu.VMEM((n_chunk * SSD_HEADS, 8, CHUNK), F32)],
        compiler_params=_params(("arbitrary", "arbitrary")),
        name="ssd_scan",
    )(xbc, rowf, colf, h0, drow)


def _l2n(x):
    return x * lax.rsqrt(jnp.sum(x * x, axis=-1, keepdims=True) + EPS)


def _dn_body(direction, seg, qkv_ref, rowf_ref, colf_ref, s0_ref, lvl_ref, o_ref, ns_ref,
             st, a_s, d_s, rhs_s, qk_s, qd_s, ke_s, u_s, w_s, gl_s):
    is_prompt = _scan_init(seg, direction, st, s0_ref)
    row = lax.broadcasted_iota(jnp.int32, (CHUNK, CHUNK), 0)
    col = lax.broadcasted_iota(jnp.int32, (CHUNK, CHUNK), 1)
    tri = (col <= row) if direction == 0 else (col >= row)
    strict = (col < row) if direction == 0 else (col > row)
    eye = jnp.where(row == col, 1.0, 0.0)
    n_chunk = SEG // CHUNK
    order = range(n_chunk) if direction == 0 else range(n_chunk - 1, -1, -1)

    for h in range(DN_HEADS):
        ch_b = CH_BETA + 8 * direction + h
        ch_g = CH_GCS + 8 * direction + h
        for c in range(n_chunk):
            p = n_chunk * h + c
            rs = slice(c * CHUNK, (c + 1) * CHUNK)
            qn = _l2n(qkv_ref[h, rs, :].astype(F32)) * (128 ** -0.5)
            kn = _l2n(qkv_ref[DN_HEADS + h, rs, :].astype(F32))
            v = qkv_ref[2 * DN_HEADS + h, rs, :].astype(F32)
            beta = colf_ref[rs, ch_b:ch_b + 1]
            gc = colf_ref[rs, ch_g:ch_g + 1]
            gr = rowf_ref[c, ch_g:ch_g + 1, :]
            glast = gr[:, CHUNK - 1:CHUNK] if direction == 0 else gr[:, 0:1]
            decay = jnp.exp(jnp.where(tri, gc - gr, -jnp.inf))
            kb = kn * beta
            kq = _dot_nt(jnp.concatenate([kb, qn], axis=0).astype(BF16), kn.astype(BF16))
            a = jnp.where(strict, kq[0:CHUNK] * decay, 0.0)
            a_s[p] = a.astype(BF16)
            d_s[p] = eye - a * lvl_ref[0].astype(F32)
            qk_s[p] = (kq[CHUNK:2 * CHUNK] * decay).astype(BF16)
            eg = jnp.exp(gc)
            rhs_s[p] = jnp.concatenate([v * beta, kb * eg], axis=1).astype(BF16)
            qd_s[p] = (qn * eg).astype(BF16)
            ke_s[p] = (kn * jnp.exp(glast - gc)).astype(BF16)
            gl_s[p] = jnp.broadcast_to(jnp.exp(glast), (8, CHUNK))

    for lvl in range(1, 7):
        for p in range(n_chunk * DN_HEADS):
            tm = d_s[p]
            tb = tm.astype(BF16)
            a_off = a_s[p] * lvl_ref[lvl]
            d_s[p] = tm - _dot(tb, _dot(a_off, tb).astype(BF16))

    for p in range(n_chunk * DN_HEADS):
        uw = _dot(d_s[p].astype(BF16), rhs_s[p])
        u_s[p] = uw[:, 0:CHUNK]
        w_s[p] = uw[:, CHUNK:2 * CHUNK].astype(BF16)

    for c in order:
        wq = [_dot(jnp.concatenate([w_s[n_chunk * h + c], qd_s[n_chunk * h + c]], axis=0),
                   st[h].astype(BF16)) for h in range(DN_HEADS)]
        for h in range(DN_HEADS):
            p = n_chunk * h + c
            vb = (u_s[p] - wq[h][0:CHUNK]).astype(BF16)
            o = wq[h][CHUNK:2 * CHUNK] + _dot(qk_s[p], vb)
            st[h] = st[h] * gl_s[p, 0:1, :] + _dot_tn(ke_s[p], vb)
            o_ref[0, h, c * CHUNK:(c + 1) * CHUNK, :] = o.astype(BF16)

    @pl.when(is_prompt)
    def _():
        ns_ref[0, 0] = st[...]


def _dn_kernel(qkv_ref, rowf_ref, colf_ref, s0_ref, lvl_ref, o_ref, ns_ref, *scratch):
    d = pl.program_id(0)
    seg = _scan_seg(d, pl.program_id(1))
    for direction in (0, 1):
        pl.when(d == direction)(functools.partial(
            _dn_body, direction, seg, qkv_ref, rowf_ref, colf_ref, s0_ref, lvl_ref, o_ref, ns_ref, *scratch))


def _sibling_masks():
    i = jnp.arange(CHUNK)[:, None]
    j = jnp.arange(CHUNK)[None, :]
    lv = [((i >> (l + 1)) == (j >> (l + 1))) & ((i >> l) != (j >> l)) for l in range(7)]
    return jnp.stack(lv).astype(BF16)


def _dn(qkv, rowf, colf, s0):
    sp = _scan_specs(DN_HEADS, CHUNK)
    n_prob = DN_HEADS * SEG // CHUNK
    mat = lambda dt: pltpu.VMEM((n_prob, CHUNK, CHUNK), dt)
    return pl.pallas_call(
        _dn_kernel,
        grid=(2, NSEG),
        in_specs=[pl.BlockSpec((24, SEG, 128), lambda d, j: (0, _scan_seg(d, j), 0)),
                  sp["rowf"], sp["colf"], sp["h0"], _const_spec((7, CHUNK, CHUNK))],
        out_specs=[pl.BlockSpec((1, DN_HEADS, SEG, 128), lambda d, j: (d, 0, _scan_seg(d, j), 0)),
                   sp["ns"]],
        out_shape=[jax.ShapeDtypeStruct((2, DN_HEADS, T, 128), BF16),
                   jax.ShapeDtypeStruct((N_PROMPT, 2, DN_HEADS, CHUNK, CHUNK), F32)],
        scratch_shapes=[pltpu.VMEM((DN_HEADS, CHUNK, CHUNK), F32),
                        mat(BF16), mat(F32),
                        pltpu.VMEM((n_prob, CHUNK, 2 * CHUNK), BF16),
                        mat(BF16), mat(BF16), mat(BF16),
                        mat(F32), mat(BF16),
                        pltpu.VMEM((n_prob, 8, CHUNK), F32)],
        compiler_params=_params(("arbitrary", "arbitrary")),
        name="dn_scan",
    )(qkv, rowf, colf, s0, _sibling_masks())


def _residual(x, m_ref, gate_row, out, pw_ref):
    gate = m_ref[0, gate_row:gate_row + 1, :]
    return x + gate * _rms(out, pw_ref[...])


def _l0_out_kernel(y_ref, z_ref, o_ref, gate_ref, xpr_ref, xsa_ref, m_ref, w_ref, snw_ref, dnw_ref, pw_ref,
                   xo_ref):
    y = (y_ref[0].astype(F32) + y_ref[1].astype(F32)) * _silu(z_ref[...].astype(F32))
    yn = _rms(y, snw_ref[...]).astype(BF16)
    heads = []
    for h in range(DN_HEADS):
        oh = o_ref[0, h].astype(F32) + o_ref[1, h].astype(F32)
        gh = gate_ref[:, 128 * h:128 * h + 128].astype(F32)
        heads.append((_rms(oh, dnw_ref[...]) * _silu(gh)).astype(BF16))
    on = jnp.concatenate(heads, axis=1)
    out = _dot(yn, w_ref[0:1024, :]) + _dot(on, w_ref[1024:2048, :])
    x = _pick_rows(pl.program_id(0), xpr_ref, xsa_ref)
    xo_ref[...] = _residual(x, m_ref, 2, out, pw_ref)


def _l0_out(y, z, o, gate, x_prompt, x_sample, modt, w_out, snw, dnw, pw):
    return pl.pallas_call(
        _l0_out_kernel,
        grid=(NSEG,),
        in_specs=[pl.BlockSpec((2, SEG, D), lambda i: (0, i, 0)), _row_spec(D),
                  pl.BlockSpec((2, DN_HEADS, SEG, 128), lambda i: (0, 0, i, 0)), _row_spec(D),
                  _prompt_spec(D), _sample_spec(D), _mod_spec(), _const_spec((2048, D)),
                  _const_spec((1, D)), _const_spec((1, 128)), _const_spec((1, D))],
        out_specs=_row_spec(D),
        out_shape=jax.ShapeDtypeStruct((T, D), F32),
        compiler_params=_params(("arbitrary",)),
        name="l0_out",
    )(y, z, o, gate, x_prompt, x_sample, modt, w_out, snw, dnw, pw)


def _l1_out_kernel(a_ref, bp_ref, bs_ref, x_ref, m_ref, w_ref, pw_ref, xo_ref):
    def finish(b_ref):
        out = _dot(a_ref[...], w_ref[0:1024, :]) + _dot(b_ref[...], w_ref[1024:2048, :])
        xo_ref[...] = _residual(x_ref[...], m_ref, 2, out, pw_ref)

    is_prompt = pl.program_id(0) < N_PROMPT
    pl.when(is_prompt)(functools.partial(finish, bp_ref))
    pl.when(jnp.logical_not(is_prompt))(functools.partial(finish, bs_ref))


def _l1_out(a, b_prompt, b_sample, x, modt, w_out, pw):
    return pl.pallas_call(
        _l1_out_kernel,
        grid=(NSEG,),
        in_specs=[_row_spec(D),
                  pl.BlockSpec((SEG, D), lambda i: (jnp.minimum(i, N_PROMPT - 1), 0)),
                  pl.BlockSpec((SEG, D), lambda i: (jnp.maximum(i - N_PROMPT, 0), 0)),
                  _row_spec(D), _mod_spec(), _const_spec((2048, D)), _const_spec((1, D))],
        out_specs=_row_spec(D),
        out_shape=jax.ShapeDtypeStruct((T, D), F32),
        compiler_params=_params(("arbitrary",)),
        name="l1_out",
    )(a, b_prompt, b_sample, x, modt, w_out, pw)


def _store_split(i, val, prompt_ref, sample_ref):
    @pl.when(i < N_PROMPT)
    def _():
        prompt_ref[...] = val

    @pl.when(i >= N_PROMPT)
    def _():
        sample_ref[...] = val


def _ffn_kernel(x_ref, xp_ref, xn_ref, m_ref, nw_ref, wu_ref, cw_ref, cb_ref, wd_ref, pw_ref,
                *out_refs):
    i = pl.program_id(0)
    shift = m_ref[0, 3:4, :]
    scale = m_ref[0, 4:5, :]
    xe, valid = _halo_rows(i, x_ref[...], xp_ref, xn_ref)
    he = (_rms(xe, nw_ref[...]) * (1.0 + scale) + shift).astype(BF16)
    n_chunk = D_FF // FF_CHUNK
    cols = lambda cc: [slice(base + cc * FF_CHUNK, base + (cc + 1) * FF_CHUNK) for base in (0, D_FF)]
    up = lambda cc: [_dot(he, wu_ref[:, cs]) for cs in cols(cc)]
    acc = jnp.zeros((SEG, D), F32)
    nxt = up(0)
    for cc in range(n_chunk):
        cur = nxt
        if cc + 1 < n_chunk:
            nxt = up(cc + 1)
        a, b = [_dwconv(p, valid, cw_ref, cb_ref, cs, 3) for p, cs in zip(cur, cols(cc))]
        g = (_silu(a) * b).astype(BF16)
        acc = acc + _dot(g, wd_ref[cc * FF_CHUNK:(cc + 1) * FF_CHUNK, :])
    xo = _residual(x_ref[...], m_ref, 5, acc, pw_ref)
    if len(out_refs) == 1:
        out_refs[0][...] = xo
    else:
        _store_split(i, xo, *out_refs)


def _ffn(x, modt, nw, wu, cw, cb, wd, pw, split_out=False):
    if split_out:
        out_specs = [_prompt_spec(D), _sample_spec(D)]
        out_shape = [jax.ShapeDtypeStruct((N_PROMPT * SEG, D), F32),
                     jax.ShapeDtypeStruct((N_SAMPLE * 2048, D), F32)]
    else:
        out_specs = _row_spec(D)
        out_shape = jax.ShapeDtypeStruct((T, D), F32)
    return pl.pallas_call(
        _ffn_kernel,
        grid=(NSEG,),
        in_specs=[_row_spec(D), _prev_spec(), _next_spec(), _mod_spec(), _const_spec((1, D)),
                  _const_spec((D, 2 * D_FF)), _const_spec((3, 2 * D_FF)), _const_spec((1, 2 * D_FF)),
                  _const_spec((D_FF, D)), _const_spec((1, D))],
        out_specs=out_specs,
        out_shape=out_shape,
        compiler_params=_params(("arbitrary",)),
        name="conv_ffn",
    )(x, x, x, modt, nw, wu, cw, cb, wd, pw)


def _l1_in_kernel(x_ref, m_ref, nw_ref, w_ref, lnw_ref, lnb_ref, ws_ref, bs_ref,
                  mlp_ref, q_ref, kp_ref, ks_ref, vp_ref, vs_ref):
    i = pl.program_id(0)
    shift = m_ref[0, 0:1, :]
    scale = m_ref[0, 1:2, :]
    h = (_rms(x_ref[...], nw_ref[...]) * (1.0 + scale) + shift).astype(BF16)
    u = jax.nn.gelu(_dot(h, w_ref[:, 0:1024]))
    gv = jax.nn.gelu(_dot(h, w_ref[:, 1024:2048]))
    gc = gv - jnp.mean(gv, axis=-1, keepdims=True)
    gn = gc * lax.rsqrt(jnp.mean(gc * gc, axis=-1, keepdims=True) + EPS) * lnw_ref[...] + lnb_ref[...]
    gb = gn.astype(BF16)
    for c in range(SEG // CHUNK):
        rs = slice(c * CHUNK, (c + 1) * CHUNK)
        for g in range(4):
            cs = slice(256 * g, 256 * g + 256)
            sv = _dot(ws_ref[g], gb[rs, cs]) + bs_ref[:, cs]
            mlp_ref[rs, cs] = (u[rs, cs] * sv).astype(BF16)
    q_ref[...] = _dot(h, w_ref[:, 2048:3072]).astype(BF16)
    _store_split(i, _dot(h, w_ref[:, 3072:3328]), kp_ref, ks_ref)
    _store_split(i, _dot(h, w_ref[:, 3328:3584]), vp_ref, vs_ref)


def _l1_in(x, modt, nw, w_in, lnw, lnb, ws, bs):
    n_p, n_s = N_PROMPT * SEG, N_SAMPLE * 2048
    return pl.pallas_call(
        _l1_in_kernel,
        grid=(NSEG,),
        in_specs=[_row_spec(D), _mod_spec(), _const_spec((1, D)), _const_spec((D, 3584)),
                  _const_spec((1, D)), _const_spec((1, D)), _const_spec((4, CHUNK, CHUNK)),
                  _const_spec((CHUNK, D))],
        out_specs=[_row_spec(D), _row_spec(D), _prompt_spec(256), _sample_spec(256),
                   _prompt_spec(256), _sample_spec(256)],
        out_shape=[jax.ShapeDtypeStruct((T, D), BF16), jax.ShapeDtypeStruct((T, D), BF16),
                   jax.ShapeDtypeStruct((n_p, 256), F32), jax.ShapeDtypeStruct((n_s, 256), F32),
                   jax.ShapeDtypeStruct((n_p, 256), F32), jax.ShapeDtypeStruct((n_s, 256), F32)],
        compiler_params=_params(("arbitrary",)),
        name="l1_in",
    )(x, modt, nw, w_in, lnw, lnb, ws, bs)


def _sink_softmax_pv(s, sink_col, vb):
    m = jnp.maximum(jnp.max(s, axis=-1, keepdims=True), sink_col)
    p = jnp.exp(s - m)
    den = jnp.sum(p, axis=-1, keepdims=True) + jnp.exp(sink_col - m)
    return _dot(p.astype(BF16), vb) / den


def _sink_column(sink_ref, kh, rows):
    parts = [jnp.broadcast_to(sink_ref[:, 4 * kh + g:4 * kh + g + 1], (rows, 1)) for g in range(4)]
    return jnp.concatenate(parts, axis=0)


def _ctx_attn_kernel(q_ref, k_ref, v_ref, sink_ref, o_ref):
    scale = 128 ** -0.5
    for kh in range(2):
        kb = k_ref[:, 128 * kh:128 * kh + 128].astype(BF16)
        vb = v_ref[:, 128 * kh:128 * kh + 128].astype(BF16)
        q4 = jnp.concatenate([q_ref[:, 128 * (4 * kh + g):128 * (4 * kh + g) + 128] for g in range(4)], axis=0)
        s = _dot_nt(q4, kb) * scale
        o = _sink_softmax_pv(s, _sink_column(sink_ref, kh, SEG), vb)
        for g in range(4):
            hd = 4 * kh + g
            o_ref[:, 128 * hd:128 * hd + 128] = o[g * SEG:(g + 1) * SEG].astype(BF16)


def _ctx_attn(q, k, v, sink_row):
    return pl.pallas_call(
        _ctx_attn_kernel,
        grid=(N_PROMPT,),
        in_specs=[_row_spec(D), _row_spec(256), _row_spec(256), _const_spec((1, 128))],
        out_specs=_row_spec(D),
        out_shape=jax.ShapeDtypeStruct((N_PROMPT * SEG, D), BF16),
        compiler_params=_params(("arbitrary",)),
        name="ctx_attn",
    )(q, k, v, sink_row)


N_QBLK = 2048 // CHUNK
N_CTX = 512


def _rope(x, cos2, sin2):
    return x * cos2 + pltpu.roll(x, 64, 1) * sin2


def _lat_attn_kernel(q_ref, kp_ref, kc_ref, kn_ref, vp_ref, vc_ref, vn_ref, ck_ref, cv_ref,
                     cosq_ref, sinq_ref, cosp_ref, sinp_ref, cosn_ref, sinn_ref, sink_ref, o_ref):
    qi = pl.program_id(1)
    scale = 128 ** -0.5
    nk = 3 * CHUNK + N_CTX
    r = lax.broadcasted_iota(jnp.int32, (4 * CHUNK, nk), 0) & (CHUNK - 1)
    c = lax.broadcasted_iota(jnp.int32, (4 * CHUNK, nk), 1)
    far = 4 * CHUNK
    is_prev = c < CHUNK
    is_next = (c >= 2 * CHUNK) & (c < 3 * CHUNK)
    ok_prev = is_prev & (c >= r + jnp.where(qi > 0, 0, far))
    ok_next = is_next & (c - 2 * CHUNK <= r - jnp.where(qi < N_QBLK - 1, 0, far))
    mask = ok_prev | ok_next | jnp.logical_not(is_prev | is_next)
    for kh in range(2):
        ks = slice(128 * kh, 128 * kh + 128)
        k_all = jnp.concatenate([
            _rope(kp_ref[:, ks], cosp_ref[...], sinp_ref[...]).astype(BF16),
            _rope(kc_ref[:, ks], cosq_ref[...], sinq_ref[...]).astype(BF16),
            _rope(kn_ref[:, ks], cosn_ref[...], sinn_ref[...]).astype(BF16),
            ck_ref[0, :, ks].astype(BF16)], axis=0)
        v_all = jnp.concatenate([vp_ref[:, ks], vc_ref[:, ks], vn_ref[:, ks], cv_ref[0, :, ks]],
                                axis=0).astype(BF16)
        q4 = jnp.concatenate([
            (_rope(q_ref[:, 128 * (4 * kh + g):128 * (4 * kh + g) + 128].astype(F32),
                   cosq_ref[...], sinq_ref[...]) * scale).astype(BF16) for g in range(4)], axis=0)
        s = jnp.where(mask, _dot_nt(q4, k_all), -jnp.inf)
        o = _sink_softmax_pv(s, _sink_column(sink_ref, kh, CHUNK), v_all)
        for g in range(4):
            hd = 4 * kh + g
            o_ref[:, 128 * hd:128 * hd + 128] = o[g * CHUNK:(g + 1) * CHUNK].astype(BF16)


def _lat_attn(q, k, v, cache_k, cache_v, cos2, sin2, sink_row):
    base = N_PROMPT * SEG // CHUNK

    def blk(b, qi):
        return b * N_QBLK + qi

    def prev_blk(b, qi):
        return b * N_QBLK + jnp.maximum(qi - 1, 0)

    def next_blk(b, qi):
        return b * N_QBLK + jnp.minimum(qi + 1, N_QBLK - 1)

    kv_spec = lambda f: pl.BlockSpec((CHUNK, 256), lambda b, qi: (f(b, qi), 0))
    tab = lambda f: pl.BlockSpec((CHUNK, 128), lambda b, qi: (f(qi), 0))
    cache_spec = pl.BlockSpec((1, N_CTX, 256), lambda b, qi: (b, 0, 0))
    same = lambda qi: qi
    before = lambda qi: jnp.maximum(qi - 1, 0)
    after = lambda qi: jnp.minimum(qi + 1, N_QBLK - 1)
    return pl.pallas_call(
        _lat_attn_kernel,
        grid=(N_SAMPLE, N_QBLK),
        in_specs=[pl.BlockSpec((CHUNK, D), lambda b, qi: (base + blk(b, qi), 0)),
                  kv_spec(prev_blk), kv_spec(blk), kv_spec(next_blk),
                  kv_spec(prev_blk), kv_spec(blk), kv_spec(next_blk),
                  cache_spec, cache_spec,
                  tab(same), tab(same), tab(before), tab(before), tab(after), tab(after),
                  pl.BlockSpec((1, 128), lambda b, qi: (0, 0))],
        out_specs=pl.BlockSpec((CHUNK, D), lambda b, qi: (blk(b, qi), 0)),
        out_shape=jax.ShapeDtypeStruct((N_SAMPLE * 2048, D), BF16),
        compiler_params=_params(("arbitrary", "arbitrary")),
        name="lat_attn",
    )(q, k, k, k, v, v, v, cache_k, cache_v, cos2, sin2, cos2, sin2, cos2, sin2, sink_row)


def _rope_tables():
    length = 2048
    rows = length // 64
    rowp = jnp.repeat(jnp.arange(rows, dtype=F32), 64)
    colp = jnp.tile(jnp.arange(64, dtype=F32), rows)
    inv = 10000.0 ** (-jnp.arange(32, dtype=F32) / 32)
    ang = jnp.concatenate([rowp[:, None] * inv, colp[:, None] * inv], axis=-1)
    cos, sin = jnp.cos(ang), jnp.sin(ang)
    return jnp.concatenate([cos, cos], axis=-1), jnp.concatenate([-sin, sin], axis=-1)


def kernel(x_prompt, x_sample, state_l0_ssd, state_l0_dn, cache_l1_k, cache_l1_v, c, c_ctx, mod_w_l0, mod_b_l0, norm_mix_pre_l0, norm_mix_post_l0, norm_ffn_pre_l0, norm_ffn_post_l0, ffn_up_l0, ffn_conv_w_l0, ffn_conv_b_l0, ffn_down_l0, mod_w_l1, mod_b_l1, norm_mix_pre_l1, norm_mix_post_l1, norm_ffn_pre_l1, norm_ffn_post_l1, ffn_up_l1, ffn_conv_w_l1, ffn_conv_b_l1, ffn_down_l1, mix_in_l0, mix_out_l0, ssd_conv_w, ssd_conv_b, ssd_dt_bias, ssd_A_log, ssd_D, ssd_norm_w, dn_conv_w, dn_dt_bias, dn_A_log, dn_norm_w, mix_in_l1, mix_out_l1, sg_ln_w, sg_ln_b, sg_w_s, sg_b_s, attn_sink):
    row = lambda a: a.reshape(1, -1).astype(F32)
    x_p = x_prompt.reshape(N_PROMPT * SEG, D)
    x_s = x_sample.reshape(N_SAMPLE * 2048, D)
    cond8 = jnp.concatenate([c_ctx[None, :], c, jnp.zeros((5, D), F32)], axis=0)
    mod0 = _adaln(cond8, mod_w_l0, mod_b_l0)
    mod1 = _adaln(cond8, mod_w_l1, mod_b_l1)

    w = mix_in_l0
    w_main = jnp.concatenate([w[:, 1024:3072], w[:, 3104:6176], w[:, 0:1024], w[:, 6176:7200],
                              w[:, 3072:3104], w[:, 7200:7232], jnp.zeros((D, 64), F32)], axis=1).astype(BF16)
    cw = jnp.concatenate([ssd_conv_w, dn_conv_w], axis=1)
    cb = jnp.concatenate([ssd_conv_b, jnp.zeros((3072,), F32)]).reshape(1, -1)
    xbc, qkv, z, gate, small = _l0_in(x_p, x_s, mod0, row(norm_mix_pre_l0), w_main, cw, cb)
    bias_col = jnp.concatenate([ssd_dt_bias.reshape(-1), dn_dt_bias.reshape(-1), jnp.zeros((80,), F32)]).reshape(128, 1)
    mult_col = jnp.concatenate([-jnp.exp(ssd_A_log.reshape(-1)), -jnp.exp(dn_A_log.reshape(-1)),
                                jnp.zeros((80,), F32)]).reshape(128, 1)
    rowf, colf = _prep(small, bias_col, mult_col)
    y, new_ssd = _ssd(xbc, rowf, colf, state_l0_ssd, jnp.repeat(ssd_D, 64).reshape(1, D))
    o, new_dn = _dn(qkv, rowf, colf, state_l0_dn)
    x = _l0_out(y, z, o, gate, x_p, x_s, mod0, mix_out_l0.astype(BF16), row(ssd_norm_w), row(dn_norm_w),
                row(norm_mix_post_l0))
    x = _ffn(x, mod0, row(norm_ffn_pre_l0), ffn_up_l0.astype(BF16), ffn_conv_w_l0, row(ffn_conv_b_l0),
             ffn_down_l0.astype(BF16), row(norm_ffn_post_l0))

    bs_full = jnp.repeat(sg_b_s.T, 256, axis=1)
    mlp, q, k_p, k_s, v_p, v_s = _l1_in(x, mod1, row(norm_mix_pre_l1), mix_in_l1.astype(BF16), row(sg_ln_w),
                                        row(sg_ln_b), sg_w_s.astype(BF16), bs_full)
    sink_row = jnp.pad(attn_sink, (0, 120)).reshape(1, 128)
    attn_p = _ctx_attn(q, k_p, v_p, sink_row)
    cos2, sin2 = _rope_tables()
    attn_s = _lat_attn(q, k_s, v_s, cache_l1_k.reshape(N_SAMPLE, N_CTX, 256),
                       cache_l1_v.reshape(N_SAMPLE, N_CTX, 256), cos2, sin2, sink_row)
    x = _l1_out(mlp, attn_p, attn_s, x, mod1, mix_out_l1.astype(BF16), row(norm_mix_post_l1))
    y_p, y_s = _ffn(x, mod1, row(norm_ffn_pre_l1), ffn_up_l1.astype(BF16), ffn_conv_w_l1, row(ffn_conv_b_l1),
                    ffn_down_l1.astype(BF16), row(norm_ffn_post_l1), split_out=True)

    return (y_p.reshape(N_PROMPT, SEG, D),
            y_s.reshape(N_SAMPLE, 2048, D),
            new_ssd,
            new_dn,
            k_p.reshape(N_PROMPT, SEG, 2, 128),
            v_p.reshape(N_PROMPT, SEG, 2, 128))
```

```python
import functools

import jax
import jax.numpy as jnp
from jax import lax
from jax.experimental import pallas as pl
from jax.experimental.pallas import tpu as pltpu

F32 = jnp.float32
BF16 = jnp.bfloat16

D = 1024
N_PROMPT = 32
SEG = 256
CHUNK = 128
SEG_PER_SAMPLE = 8
N_SAMPLE = 2
NSEG = N_PROMPT + N_SAMPLE * SEG_PER_SAMPLE
T = NSEG * SEG
HALO = 8
EPS = 1e-6
VMEM_LIMIT = 56 * 1024 * 1024

SSD_HEADS = 16
DN_HEADS = 8
D_FF = 2816
FF_CHUNK = 256

CH_DT = 0
CH_BETA = 48
CH_ACS = 64
CH_GCS = 96


def _dot(a, b):
    return jnp.dot(a, b, preferred_element_type=F32)


def _dot_nt(a, b):
    return lax.dot_general(a, b, (((1,), (1,)), ((), ())), preferred_element_type=F32)


def _dot_tn(a, b):
    return lax.dot_general(a, b, (((0,), (0,)), ((), ())), preferred_element_type=F32)


def _silu(x):
    return x * jax.nn.sigmoid(x)


def _rms(x, w):
    return x * lax.rsqrt(jnp.mean(x * x, axis=-1, keepdims=True) + EPS) * w


TOK = 512
NTOK_PROMPT = N_PROMPT * SEG // TOK
TOK_PER_SAMPLE = 2048 // TOK
NTOK = T // TOK


def _mod_index(i):
    return jnp.where(i < NTOK_PROMPT, 0, 1 + jnp.maximum(i - NTOK_PROMPT, 0) // TOK_PER_SAMPLE)


def _halo_flags(i):
    r = lax.rem(jnp.maximum(i - NTOK_PROMPT, 0), TOK_PER_SAMPLE)
    in_sample = i >= NTOK_PROMPT
    has_prev = jnp.where(jnp.logical_and(in_sample, r != 0), 1.0, 0.0)
    has_next = jnp.where(jnp.logical_and(in_sample, r != TOK_PER_SAMPLE - 1), 1.0, 0.0)
    return has_prev, has_next


def _row_spec(width):
    return pl.BlockSpec((TOK, width), lambda i: (i, 0))


def _prev_spec(first_tile=0, n_rows=T):
    return pl.BlockSpec((HALO, D), lambda i: (
        jnp.clip((i - first_tile) * (TOK // HALO) - 1, 0, n_rows // HALO - 1), 0))


def _next_spec(first_tile=0, n_rows=T):
    return pl.BlockSpec((HALO, D), lambda i: (
        jnp.clip((i - first_tile + 1) * (TOK // HALO), 0, n_rows // HALO - 1), 0))


def _prompt_spec(width):
    return pl.BlockSpec((TOK, width), lambda i: (jnp.minimum(i, NTOK_PROMPT - 1), 0))


def _sample_spec(width):
    return pl.BlockSpec((TOK, width), lambda i: (jnp.maximum(i - NTOK_PROMPT, 0), 0))


def _mod_spec():
    return pl.BlockSpec((1, 8, D), lambda i: (_mod_index(i), 0, 0))


def _const_spec(shape):
    nd = len(shape)
    return pl.BlockSpec(shape, lambda *_: (0,) * nd, pipeline_mode=pl.Buffered(1))


def _params(sem):
    return pltpu.CompilerParams(dimension_semantics=sem, vmem_limit_bytes=VMEM_LIMIT)


def _adaln_kernel(c_ref, w_ref, b_ref, o_ref):
    s = _silu(c_ref[...]).astype(BF16)
    o_ref[...] = _dot(s, w_ref[...].astype(BF16)) + b_ref[...]


def _adaln(cond8, w, b):
    tn = 768
    out = pl.pallas_call(
        _adaln_kernel,
        grid=(6 * D // tn,),
        in_specs=[pl.BlockSpec((8, D), lambda j: (0, 0)),
                  pl.BlockSpec((D, tn), lambda j: (0, j)),
                  pl.BlockSpec((1, tn), lambda j: (0, j))],
        out_specs=pl.BlockSpec((8, tn), lambda j: (0, j)),
        out_shape=jax.ShapeDtypeStruct((8, 6 * D), F32),
        compiler_params=_params(("arbitrary",)),
        name="adaln",
    )(cond8, w, b.reshape(1, -1))
    mod = out.reshape(8, 6, D)[:3]
    return jnp.pad(mod, ((0, 0), (0, 2), (0, 0)))


L0_CONV = 5120
L0_COLS = 7296
CONV_CHUNK = 512


def _pick_rows(i, prompt_ref, sample_ref):
    return jnp.where(i < NTOK_PROMPT, prompt_ref[...], sample_ref[...])


def _halo_rows(i, x, xp_ref, xn_ref):
    xe = jnp.concatenate([x, xn_ref[...], xp_ref[...]], axis=0)
    has_prev, has_next = _halo_flags(i)
    r = lax.broadcasted_iota(jnp.int32, (2 * HALO, 1), 0)
    return xe, (jnp.where(r < HALO, has_next, has_prev), jnp.where(i < NTOK_PROMPT, 1.0, 0.0))


def _dwconv(p, tile_info, w_ref, b_ref, cs, width):
    valid, two_seq = tile_info
    n = p.shape[0]
    p = jnp.concatenate([p[0:TOK], p[TOK:n] * valid], axis=0)
    half = width // 2
    taps = [p if k == half else pltpu.roll(p, (half - k) % n, 0) for k in range(width)]
    acc = b_ref[:, cs]
    for k in range(width):
        acc = acc + taps[k][0:TOK] * w_ref[k:k + 1, cs]
    lo, hi = SEG - 8, SEG + 8
    j = lax.broadcasted_iota(jnp.int32, (16, 1), 0)
    fix = b_ref[:, cs]
    for k in range(width):
        src = j + (k - half)
        crosses = jnp.where(j < 8, jnp.where(src >= 8, 1.0, 0.0), jnp.where(src < 8, 1.0, 0.0))
        keep = 1.0 - two_seq * crosses
        fix = fix + taps[k][lo:hi] * keep * w_ref[k:k + 1, cs]
    return jnp.concatenate([acc[0:lo], fix, acc[hi:TOK]], axis=0)


def _l0_in_kernel(zero_ref, xpr_ref, xsa_ref, xp_ref, xn_ref, m_ref, nw_ref, w_ref, cw_ref, cb_ref,
                  xbc_ref, qkv_ref, z_ref, gate_ref, small_ref, p_s):
    i = pl.program_id(0)
    shift = m_ref[0, 0:1, :]
    scale = m_ref[0, 1:2, :]
    xe, valid = _halo_rows(i, _pick_rows(i, xpr_ref, xsa_ref), xp_ref, xn_ref)
    he = (_rms(xe, nw_ref[...]) * (1.0 + scale) + shift).astype(BF16)
    hc = he[0:TOK]

    def plain(j):
        if j < 4:
            z_ref[:, 256 * j:256 * j + 256] = _dot(hc, w_ref[:, 5120 + 256 * j:5376 + 256 * j]).astype(BF16)
        elif j < 8:
            j -= 4
            gate_ref[:, 256 * j:256 * j + 256] = _dot(hc, w_ref[:, 6144 + 256 * j:6400 + 256 * j]).astype(BF16)
        elif j == 8:
            small_ref[...] = _dot(hc, w_ref[:, 7168:7296])

    z = zero_ref[0]
    n_conv = L0_CONV // CONV_CHUNK
    conv_cols = lambda cc: slice(cc * CONV_CHUNK, (cc + 1) * CONV_CHUNK)
    p_s[z] = _dot(he, w_ref[:, conv_cols(0)])
    for cc in range(n_conv):
        cur = p_s[cc % 2 + z]
        if cc + 1 < n_conv:
            p_s[(cc + 1) % 2 + z] = _dot(he, w_ref[:, conv_cols(cc + 1)])
        plain(cc)
        cs = conv_cols(cc)
        val = _silu(_dwconv(cur, valid, cw_ref, cb_ref, cs, 5)).astype(BF16)
        if cc < 4:
            xbc_ref[:, cs] = val
        else:
            for t in range(4):
                qkv_ref[(cc - 4) * 4 + t] = val[:, t * 128:(t + 1) * 128]


def _l0_in(x_prompt, x_sample, modt, nw, w_main, cw, cb):
    n_s = N_SAMPLE * 2048
    return pl.pallas_call(
        _l0_in_kernel,
        grid=(NTOK,),
        in_specs=[pl.BlockSpec(memory_space=pltpu.SMEM), _prompt_spec(D), _sample_spec(D),
                  _prev_spec(NTOK_PROMPT, n_s), _next_spec(NTOK_PROMPT, n_s), _mod_spec(),
                  _const_spec((1, D)), _const_spec((D, L0_COLS)),
                  _const_spec((5, L0_CONV)), _const_spec((1, L0_CONV))],
        out_specs=[_row_spec(2048),
                   pl.BlockSpec((24, TOK, 128), lambda i: (0, i, 0)),
                   _row_spec(D), _row_spec(D), _row_spec(128)],
        out_shape=[jax.ShapeDtypeStruct((T, 2048), BF16),
                   jax.ShapeDtypeStruct((24, T, 128), BF16),
                   jax.ShapeDtypeStruct((T, D), BF16),
                   jax.ShapeDtypeStruct((T, D), BF16),
                   jax.ShapeDtypeStruct((T, 128), F32)],
        scratch_shapes=[pltpu.VMEM((2, TOK + 2 * HALO, CONV_CHUNK), F32)],
        compiler_params=_params(("arbitrary",)),
        name="l0_in",
    )(jnp.zeros((1,), jnp.int32), x_prompt, x_sample, x_sample, x_sample, modt, nw, w_main, cw, cb)


def _split3_dot(a, tri):
    a1 = a.astype(BF16)
    r1 = a - a1.astype(F32)
    a2 = r1.astype(BF16)
    a3 = (r1 - a2.astype(F32)).astype(BF16)
    return _dot(a1, tri) + _dot(a2, tri) + _dot(a3, tri)


def _prep_kernel(s_ref, bias_ref, mult_ref, rowf_ref, colf_ref):
    row = lax.broadcasted_iota(jnp.int32, (CHUNK, CHUNK), 0)
    col = lax.broadcasted_iota(jnp.int32, (CHUNK, CHUNK), 1)
    upper = jnp.where(row <= col, 1.0, 0.0).astype(BF16)
    lower = jnp.where(row >= col, 1.0, 0.0).astype(BF16)
    backward = ((row >= 16) & (row < 32)) | ((row >= 40) & (row < 48))
    for c in range(TOK // CHUNK):
        rs = slice(c * CHUNK, (c + 1) * CHUNK)
        raw_t = s_ref[rs, :].T
        xb = raw_t + bias_ref[...]
        sp = jnp.maximum(xb, 0.0) + jnp.log1p(jnp.exp(-jnp.abs(xb)))
        vals = jnp.where(row < 48, sp, jax.nn.sigmoid(raw_t))
        a = jnp.where(row < 48, sp * mult_ref[...], 0.0)
        cum = jnp.where(backward, _split3_dot(a, lower), _split3_dot(a, upper))
        out = jnp.concatenate([vals[0:64], cum[0:64]], axis=0)
        rowf_ref[c] = out
        colf_ref[rs, :] = out.T


def _prep(small, bias_col, mult_col):
    return pl.pallas_call(
        _prep_kernel,
        grid=(NTOK,),
        in_specs=[_row_spec(128), _const_spec((128, 1)), _const_spec((128, 1))],
        out_specs=[pl.BlockSpec((TOK // CHUNK, CHUNK, CHUNK), lambda i: (i, 0, 0)), _row_spec(128)],
        out_shape=[jax.ShapeDtypeStruct((T // CHUNK, CHUNK, CHUNK), F32),
                   jax.ShapeDtypeStruct((T, 128), F32)],
        compiler_params=_params(("arbitrary",)),
        name="scan_prep",
    )(small, bias_col, mult_col)


def _scan_seg(d, j):
    return j + d * (NSEG - 1 - 2 * j)


def _scan_batch(seg):
    return jnp.maximum(seg - N_PROMPT, 0) // SEG_PER_SAMPLE


def _scan_specs(heads, rows):
    seg_map = lambda d, j: _scan_seg(d, j)
    return dict(
        rowf=pl.BlockSpec((2, CHUNK, CHUNK), lambda d, j: (seg_map(d, j), 0, 0)),
        colf=pl.BlockSpec((SEG, 128), lambda d, j: (seg_map(d, j), 0)),
        h0=pl.BlockSpec((1, 1, heads, rows, 128), lambda d, j: (_scan_batch(seg_map(d, j)), d, 0, 0, 0)),
        ns=pl.BlockSpec((1, 1, heads, rows, 128),
                        lambda d, j: (jnp.minimum(seg_map(d, j), N_PROMPT - 1), d, 0, 0, 0)),
    )


def _scan_init(seg, direction, st, h0_ref):
    is_prompt = seg < N_PROMPT
    r = lax.rem(jnp.maximum(seg - N_PROMPT, 0), SEG_PER_SAMPLE)
    first = 0 if direction == 0 else SEG_PER_SAMPLE - 1

    @pl.when(is_prompt)
    def _():
        st[...] = jnp.zeros_like(st)

    @pl.when(jnp.logical_and(jnp.logical_not(is_prompt), r == first))
    def _():
        st[...] = h0_ref[0, 0].reshape(st.shape)

    return is_prompt


def _ssd_body(direction, seg, xbc_ref, rowf_ref, colf_ref, h0_ref, drow_ref, y_ref, ns_ref,
              st, m_s, e_s, w_s, gl_s):
    is_prompt = _scan_init(seg, direction, st, h0_ref)
    row = lax.broadcasted_iota(jnp.int32, (CHUNK, CHUNK), 0)
    col = lax.broadcasted_iota(jnp.int32, (CHUNK, CHUNK), 1)
    mask = (col <= row) if direction == 0 else (col >= row)
    lane = lax.broadcasted_iota(jnp.int32, (CHUNK, 256), 1)
    heads = [(lane >= 64 * hh) & (lane < 64 * hh + 64) for hh in range(4)]
    n_chunk = SEG // CHUNK
    order = range(n_chunk) if direction == 0 else range(n_chunk - 1, -1, -1)

    for c in range(n_chunk):
        rs = slice(c * CHUNK, (c + 1) * CHUNK)
        for g in range(4):
            bg = xbc_ref[rs, 1024 + 128 * g:1152 + 128 * g]
            cg = xbc_ref[rs, 1536 + 128 * g:1664 + 128 * g]
            gmat = _dot_nt(cg, bg)
            e_in = jnp.zeros((CHUNK, 256), F32)
            w_out = jnp.zeros((CHUNK, 256), F32)
            for hh in range(4):
                h = 4 * g + hh
                ch_dt = CH_DT + 16 * direction + h
                ch_ac = CH_ACS + 16 * direction + h
                ar = rowf_ref[c, ch_ac:ch_ac + 1, :]
                dt_r = rowf_ref[c, ch_dt:ch_dt + 1, :]
                last = ar[:, CHUNK - 1:CHUNK] if direction == 0 else ar[:, 0:1]
                ac_b = jnp.broadcast_to(colf_ref[rs, ch_ac:ch_ac + 1], (CHUNK, CHUNK))
                dt_b = jnp.broadcast_to(colf_ref[rs, ch_dt:ch_dt + 1], (CHUNK, CHUNK))
                lmat = jnp.exp(jnp.where(mask, ac_b - ar, -jnp.inf))
                m_s[16 * c + h] = (gmat * lmat * dt_r).astype(BF16)
                e_b = jnp.exp(ac_b)
                w_b = dt_b * jnp.exp(last - ac_b)
                e_in = jnp.where(heads[hh], jnp.concatenate([e_b, e_b], axis=1), e_in)
                w_out = jnp.where(heads[hh], jnp.concatenate([w_b, w_b], axis=1), w_out)
                gl_s[16 * c + h] = jnp.broadcast_to(jnp.exp(last), (8, CHUNK))
            e_s[4 * c + g] = e_in
            w_s[4 * c + g] = w_out

    for c in order:
        rs = slice(c * CHUNK, (c + 1) * CHUNK)
        for g in range(4):
            bg = xbc_ref[rs, 1024 + 128 * g:1152 + 128 * g]
            cg = xbc_ref[rs, 1536 + 128 * g:1664 + 128 * g]
            xf = xbc_ref[rs, 256 * g:256 * g + 256].astype(F32)
            y = _dot_nt(cg, st[256 * g:256 * g + 256, :].astype(BF16)) * e_s[4 * c + g]
            for hh in range(4):
                y = y + _dot(m_s[16 * c + 4 * g + hh], jnp.where(heads[hh], xf, 0.0).astype(BF16))
            if direction == 0:
                y = y + xf * drow_ref[:, 256 * g:256 * g + 256]
            y_ref[0, rs, 256 * g:256 * g + 256] = y.astype(BF16)
            upd = _dot_tn((xf * w_s[4 * c + g]).astype(BF16), bg)
            for hh in range(4):
                hr = slice(256 * g + 64 * hh, 256 * g + 64 * hh + 64)
                st[hr, :] = st[hr, :] * gl_s[16 * c + 4 * g + hh, 0:1, :] + upd[64 * hh:64 * hh + 64]

    @pl.when(is_prompt)
    def _():
        ns_ref[0, 0] = st[...].reshape(SSD_HEADS, 64, 128)


def _ssd_kernel(xbc_ref, rowf_ref, colf_ref, h0_ref, drow_ref, y_ref, ns_ref, *scratch):
    d = pl.program_id(0)
    seg = _scan_seg(d, pl.program_id(1))
    for direction in (0, 1):
        pl.when(d == direction)(functools.partial(
            _ssd_body, direction, seg, xbc_ref, rowf_ref, colf_ref, h0_ref, drow_ref, y_ref, ns_ref, *scratch))


def _ssd(xbc, rowf, colf, h0, drow):
    sp = _scan_specs(SSD_HEADS, 64)
    n_chunk = SEG // CHUNK
    return pl.pallas_call(
        _ssd_kernel,
        grid=(2, NSEG),
        in_specs=[pl.BlockSpec((SEG, 2048), lambda d, j: (_scan_seg(d, j), 0)),
                  sp["rowf"], sp["colf"], sp["h0"],
                  pl.BlockSpec((1, D), lambda d, j: (0, 0))],
        out_specs=[pl.BlockSpec((1, SEG, D), lambda d, j: (d, _scan_seg(d, j), 0)), sp["ns"]],
        out_shape=[jax.ShapeDtypeStruct((2, T, D), BF16),
                   jax.ShapeDtypeStruct((N_PROMPT, 2, SSD_HEADS, 64, 128), F32)],
        scratch_shapes=[pltpu.VMEM((1024, 128), F32),
                        pltpu.VMEM((n_chunk * SSD_HEADS, CHUNK, CHUNK), BF16),
                        pltpu.VMEM((n_chunk * 4, CHUNK, 256), F32),
                        pltpu.VMEM((n_chunk * 4, CHUNK, 256), F32),
                        pltpu.VMEM((n_chunk * SSD_HEADS, 8, CHUNK), F32)],
        compiler_params=_params(("arbitrary", "arbitrary")),
        name="ssd_scan",
    )(xbc, rowf, colf, h0, drow)


def _l2n(x):
    return x * lax.rsqrt(jnp.sum(x * x, axis=-1, keepdims=True) + EPS)


def _dn_body(direction, seg, qkv_ref, rowf_ref, colf_ref, s0_ref, lvl_ref, o_ref, ns_ref,
             st, a_s, d_s, rhs_s, qk_s, qd_s, ke_s, u_s, w_s, gl_s):
    is_prompt = _scan_init(seg, direction, st, s0_ref)
    row = lax.broadcasted_iota(jnp.int32, (CHUNK, CHUNK), 0)
    col = lax.broadcasted_iota(jnp.int32, (CHUNK, CHUNK), 1)
    tri = (col <= row) if direction == 0 else (col >= row)
    strict = (col < row) if direction == 0 else (col > row)
    eye = jnp.where(row == col, 1.0, 0.0)
    n_chunk = SEG // CHUNK
    order = range(n_chunk) if direction == 0 else range(n_chunk - 1, -1, -1)

    def operands(h, c):
        ch_b = CH_BETA + 8 * direction + h
        ch_g = CH_GCS + 8 * direction + h
        p = n_chunk * h + c
        rs = slice(c * CHUNK, (c + 1) * CHUNK)
        qn = _l2n(qkv_ref[h, rs, :].astype(F32)) * (128 ** -0.5)
        kn = _l2n(qkv_ref[DN_HEADS + h, rs, :].astype(F32))
        v = qkv_ref[2 * DN_HEADS + h, rs, :].astype(F32)
        beta = colf_ref[rs, ch_b:ch_b + 1]
        gc = colf_ref[rs, ch_g:ch_g + 1]
        gr = rowf_ref[c, ch_g:ch_g + 1, :]
        glast = gr[:, CHUNK - 1:CHUNK] if direction == 0 else gr[:, 0:1]
        decay = jnp.exp(jnp.where(tri, gc - gr, -jnp.inf))
        kb = kn * beta
        kq = _dot_nt(jnp.concatenate([kb, qn], axis=0).astype(BF16), kn.astype(BF16))
        a = jnp.where(strict, kq[0:CHUNK] * decay, 0.0)
        a_s[p] = a.astype(BF16)
        d_s[p] = eye - a * lvl_ref[0].astype(F32)
        qk_s[p] = (kq[CHUNK:2 * CHUNK] * decay).astype(BF16)
        eg = jnp.exp(gc)
        rhs_s[p] = jnp.concatenate([v * beta, kb * eg], axis=1).astype(BF16)
        qd_s[p] = (qn * eg).astype(BF16)
        ke_s[p] = (kn * jnp.exp(glast - gc)).astype(BF16)
        gl_s[p] = jnp.broadcast_to(jnp.exp(glast), (8, CHUNK))

    def double(lvl, p):
        tm = d_s[p]
        tb = tm.astype(BF16)
        a_off = a_s[p] * lvl_ref[lvl]
        d_s[p] = tm - _dot(tb, _dot(a_off, tb).astype(BF16))

    for h in range(DN_HEADS):
        for c in range(n_chunk):
            operands(h, c)
    for lvl in range(1, 7):
        for p in range(n_chunk * DN_HEADS):
            double(lvl, p)

    for p in range(n_chunk * DN_HEADS):
        uw = _dot(d_s[p].astype(BF16), rhs_s[p])
        u_s[p] = uw[:, 0:CHUNK]
        w_s[p] = uw[:, CHUNK:2 * CHUNK].astype(BF16)

    for c in order:
        wq = [_dot(jnp.concatenate([w_s[n_chunk * h + c], qd_s[n_chunk * h + c]], axis=0),
                   st[h].astype(BF16)) for h in range(DN_HEADS)]
        for h in range(DN_HEADS):
            p = n_chunk * h + c
            vb = (u_s[p] - wq[h][0:CHUNK]).astype(BF16)
            o = wq[h][CHUNK:2 * CHUNK] + _dot(qk_s[p], vb)
            st[h] = st[h] * gl_s[p, 0:1, :] + _dot_tn(ke_s[p], vb)
            o_ref[0, h, c * CHUNK:(c + 1) * CHUNK, :] = o.astype(BF16)

    @pl.when(is_prompt)
    def _():
        ns_ref[0, 0] = st[...]


def _dn_kernel(qkv_ref, rowf_ref, colf_ref, s0_ref, lvl_ref, o_ref, ns_ref, *scratch):
    d = pl.program_id(0)
    seg = _scan_seg(d, pl.program_id(1))
    for direction in (0, 1):
        pl.when(d == direction)(functools.partial(
            _dn_body, direction, seg, qkv_ref, rowf_ref, colf_ref, s0_ref, lvl_ref, o_ref, ns_ref, *scratch))


def _sibling_masks():
    i = jnp.arange(CHUNK)[:, None]
    j = jnp.arange(CHUNK)[None, :]
    lv = [((i >> (l + 1)) == (j >> (l + 1))) & ((i >> l) != (j >> l)) for l in range(7)]
    return jnp.stack(lv).astype(BF16)


def _dn(qkv, rowf, colf, s0):
    sp = _scan_specs(DN_HEADS, CHUNK)
    n_prob = DN_HEADS * SEG // CHUNK
    mat = lambda dt: pltpu.VMEM((n_prob, CHUNK, CHUNK), dt)
    return pl.pallas_call(
        _dn_kernel,
        grid=(2, NSEG),
        in_specs=[pl.BlockSpec((24, SEG, 128), lambda d, j: (0, _scan_seg(d, j), 0)),
                  sp["rowf"], sp["colf"], sp["h0"], _const_spec((7, CHUNK, CHUNK))],
        out_specs=[pl.BlockSpec((1, DN_HEADS, SEG, 128), lambda d, j: (d, 0, _scan_seg(d, j), 0)),
                   sp["ns"]],
        out_shape=[jax.ShapeDtypeStruct((2, DN_HEADS, T, 128), BF16),
                   jax.ShapeDtypeStruct((N_PROMPT, 2, DN_HEADS, CHUNK, CHUNK), F32)],
        scratch_shapes=[pltpu.VMEM((DN_HEADS, CHUNK, CHUNK), F32),
                        mat(BF16), mat(F32),
                        pltpu.VMEM((n_prob, CHUNK, 2 * CHUNK), BF16),
                        mat(BF16), mat(BF16), mat(BF16),
                        mat(F32), mat(BF16),
                        pltpu.VMEM((n_prob, 8, CHUNK), F32)],
        compiler_params=_params(("arbitrary", "arbitrary")),
        name="dn_scan",
    )(qkv, rowf, colf, s0, _sibling_masks())


def _residual(x, m_ref, gate_row, out, pw_ref):
    gate = m_ref[0, gate_row:gate_row + 1, :]
    return x + gate * _rms(out, pw_ref[...])


def _l0_out_kernel(y_ref, z_ref, o_ref, gate_ref, xpr_ref, xsa_ref, m_ref, w_ref, snw_ref, dnw_ref, pw_ref,
                   xo_ref):
    y = (y_ref[0].astype(F32) + y_ref[1].astype(F32)) * _silu(z_ref[...].astype(F32))
    yn = _rms(y, snw_ref[...]).astype(BF16)
    heads = []
    for h in range(DN_HEADS):
        oh = o_ref[0, h].astype(F32) + o_ref[1, h].astype(F32)
        gh = gate_ref[:, 128 * h:128 * h + 128].astype(F32)
        heads.append((_rms(oh, dnw_ref[...]) * _silu(gh)).astype(BF16))
    on = jnp.concatenate(heads, axis=1)
    out = _dot(yn, w_ref[0:1024, :]) + _dot(on, w_ref[1024:2048, :])
    x = _pick_rows(pl.program_id(0), xpr_ref, xsa_ref)
    xo_ref[...] = _residual(x, m_ref, 2, out, pw_ref)


def _l0_out(y, z, o, gate, x_prompt, x_sample, modt, w_out, snw, dnw, pw):
    return pl.pallas_call(
        _l0_out_kernel,
        grid=(NTOK,),
        in_specs=[pl.BlockSpec((2, TOK, D), lambda i: (0, i, 0)), _row_spec(D),
                  pl.BlockSpec((2, DN_HEADS, TOK, 128), lambda i: (0, 0, i, 0)), _row_spec(D),
                  _prompt_spec(D), _sample_spec(D), _mod_spec(), _const_spec((2048, D)),
                  _const_spec((1, D)), _const_spec((1, 128)), _const_spec((1, D))],
        out_specs=_row_spec(D),
        out_shape=jax.ShapeDtypeStruct((T, D), F32),
        compiler_params=_params(("arbitrary",)),
        name="l0_out",
    )(y, z, o, gate, x_prompt, x_sample, modt, w_out, snw, dnw, pw)


def _l1_out_kernel(a_ref, bp_ref, bs_ref, x_ref, m_ref, w_ref, pw_ref, xo_ref):
    def finish(b_ref):
        out = _dot(a_ref[...], w_ref[0:1024, :]) + _dot(b_ref[...], w_ref[1024:2048, :])
        xo_ref[...] = _residual(x_ref[...], m_ref, 2, out, pw_ref)

    is_prompt = pl.program_id(0) < NTOK_PROMPT
    pl.when(is_prompt)(functools.partial(finish, bp_ref))
    pl.when(jnp.logical_not(is_prompt))(functools.partial(finish, bs_ref))


def _l1_out(a, b_prompt, b_sample, x, modt, w_out, pw):
    return pl.pallas_call(
        _l1_out_kernel,
        grid=(NTOK,),
        in_specs=[_row_spec(D), _prompt_spec(D), _sample_spec(D),
                  _row_spec(D), _mod_spec(), _const_spec((2048, D)), _const_spec((1, D))],
        out_specs=_row_spec(D),
        out_shape=jax.ShapeDtypeStruct((T, D), F32),
        compiler_params=_params(("arbitrary",)),
        name="l1_out",
    )(a, b_prompt, b_sample, x, modt, w_out, pw)


def _store_split(i, val, prompt_ref, sample_ref):
    @pl.when(i < NTOK_PROMPT)
    def _():
        prompt_ref[...] = val

    @pl.when(i >= NTOK_PROMPT)
    def _():
        sample_ref[...] = val


def _ffn_kernel(zero_ref, x_ref, xp_ref, xn_ref, m_ref, nw_ref, wu_ref, cw_ref, cb_ref, wd_ref, pw_ref,
                *out_and_scratch):
    *out_refs, p_s = out_and_scratch
    i = pl.program_id(0)
    z = zero_ref[0]
    shift = m_ref[0, 3:4, :]
    scale = m_ref[0, 4:5, :]
    xe, valid = _halo_rows(i, x_ref[...], xp_ref, xn_ref)
    he = (_rms(xe, nw_ref[...]) * (1.0 + scale) + shift).astype(BF16)
    n_chunk = D_FF // FF_CHUNK
    cols = lambda cc: [slice(base + cc * FF_CHUNK, base + (cc + 1) * FF_CHUNK) for base in (0, D_FF)]

    def up(cc):
        for half, cs in enumerate(cols(cc)):
            p_s[cc % 2 + z, half] = _dot(he, wu_ref[:, cs])

    acc = jnp.zeros((TOK, D), F32)
    up(0)
    for cc in range(n_chunk):
        cur = [p_s[cc % 2 + z, half] for half in range(2)]
        if cc + 1 < n_chunk:
            up(cc + 1)
        a, b = [_dwconv(p, valid, cw_ref, cb_ref, cs, 3) for p, cs in zip(cur, cols(cc))]
        g = (_silu(a) * b).astype(BF16)
        acc = acc + _dot(g, wd_ref[cc * FF_CHUNK:(cc + 1) * FF_CHUNK, :])
    xo = _residual(x_ref[...], m_ref, 5, acc, pw_ref)
    if len(out_refs) == 1:
        out_refs[0][...] = xo
    else:
        _store_split(i, xo, *out_refs)


def _ffn(x, modt, nw, wu, cw, cb, wd, pw, split_out=False):
    if split_out:
        out_specs = [_prompt_spec(D), _sample_spec(D)]
        out_shape = [jax.ShapeDtypeStruct((N_PROMPT * SEG, D), F32),
                     jax.ShapeDtypeStruct((N_SAMPLE * 2048, D), F32)]
    else:
        out_specs = _row_spec(D)
        out_shape = jax.ShapeDtypeStruct((T, D), F32)
    return pl.pallas_call(
        _ffn_kernel,
        grid=(NTOK,),
        in_specs=[pl.BlockSpec(memory_space=pltpu.SMEM),
                  _row_spec(D), _prev_spec(), _next_spec(), _mod_spec(), _const_spec((1, D)),
                  _const_spec((D, 2 * D_FF)), _const_spec((3, 2 * D_FF)), _const_spec((1, 2 * D_FF)),
                  _const_spec((D_FF, D)), _const_spec((1, D))],
        out_specs=out_specs,
        out_shape=out_shape,
        scratch_shapes=[pltpu.VMEM((2, 2, TOK + 2 * HALO, FF_CHUNK), F32)],
        compiler_params=_params(("arbitrary",)),
        name="conv_ffn",
    )(jnp.zeros((1,), jnp.int32), x, x, x, modt, nw, wu, cw, cb, wd, pw)


def _l1_in_kernel(x_ref, m_ref, nw_ref, w_ref, lnw_ref, lnb_ref, ws_ref, bs_ref,
                  mlp_ref, q_ref, kp_ref, ks_ref, vp_ref, vs_ref):
    i = pl.program_id(0)
    shift = m_ref[0, 0:1, :]
    scale = m_ref[0, 1:2, :]
    h = (_rms(x_ref[...], nw_ref[...]) * (1.0 + scale) + shift).astype(BF16)
    u = jax.nn.gelu(_dot(h, w_ref[:, 0:1024]))
    gv = jax.nn.gelu(_dot(h, w_ref[:, 1024:2048]))
    gc = gv - jnp.mean(gv, axis=-1, keepdims=True)
    gn = gc * lax.rsqrt(jnp.mean(gc * gc, axis=-1, keepdims=True) + EPS) * lnw_ref[...] + lnb_ref[...]
    gb = gn.astype(BF16)
    for c in range(TOK // CHUNK):
        rs = slice(c * CHUNK, (c + 1) * CHUNK)
        for g in range(4):
            cs = slice(256 * g, 256 * g + 256)
            sv = _dot(ws_ref[g], gb[rs, cs]) + bs_ref[:, cs]
            mlp_ref[rs, cs] = (u[rs, cs] * sv).astype(BF16)
    q_ref[...] = _dot(h, w_ref[:, 2048:3072]).astype(BF16)
    _store_split(i, _dot(h, w_ref[:, 3072:3328]), kp_ref, ks_ref)
    _store_split(i, _dot(h, w_ref[:, 3328:3584]), vp_ref, vs_ref)


def _l1_in(x, modt, nw, w_in, lnw, lnb, ws, bs):
    n_p, n_s = N_PROMPT * SEG, N_SAMPLE * 2048
    return pl.pallas_call(
        _l1_in_kernel,
        grid=(NTOK,),
        in_specs=[_row_spec(D), _mod_spec(), _const_spec((1, D)), _const_spec((D, 3584)),
                  _const_spec((1, D)), _const_spec((1, D)), _const_spec((4, CHUNK, CHUNK)),
                  _const_spec((CHUNK, D))],
        out_specs=[_row_spec(D), _row_spec(D), _prompt_spec(256), _sample_spec(256),
                   _prompt_spec(256), _sample_spec(256)],
        out_shape=[jax.ShapeDtypeStruct((T, D), BF16), jax.ShapeDtypeStruct((T, D), BF16),
                   jax.ShapeDtypeStruct((n_p, 256), F32), jax.ShapeDtypeStruct((n_s, 256), F32),
                   jax.ShapeDtypeStruct((n_p, 256), F32), jax.ShapeDtypeStruct((n_s, 256), F32)],
        compiler_params=_params(("arbitrary",)),
        name="l1_in",
    )(x, modt, nw, w_in, lnw, lnb, ws, bs)


def _sink_softmax_pv(s, sink_col, vb):
    m = jnp.maximum(jnp.max(s, axis=-1, keepdims=True), sink_col)
    p = jnp.exp(s - m)
    den = jnp.sum(p, axis=-1, keepdims=True) + jnp.exp(sink_col - m)
    return _dot(p.astype(BF16), vb) / den


def _sink_column(sink_ref, kh, rows):
    parts = [jnp.broadcast_to(sink_ref[:, 4 * kh + g:4 * kh + g + 1], (rows, 1)) for g in range(4)]
    return jnp.concatenate(parts, axis=0)


def _ctx_attn_kernel(q_ref, k_ref, v_ref, sink_ref, o_ref):
    scale = 128 ** -0.5
    for kh in range(2):
        kb = k_ref[:, 128 * kh:128 * kh + 128].astype(BF16)
        vb = v_ref[:, 128 * kh:128 * kh + 128].astype(BF16)
        q4 = jnp.concatenate([q_ref[:, 128 * (4 * kh + g):128 * (4 * kh + g) + 128] for g in range(4)], axis=0)
        s = _dot_nt(q4, kb) * scale
        o = _sink_softmax_pv(s, _sink_column(sink_ref, kh, SEG), vb)
        for g in range(4):
            hd = 4 * kh + g
            o_ref[:, 128 * hd:128 * hd + 128] = o[g * SEG:(g + 1) * SEG].astype(BF16)


def _ctx_attn(q, k, v, sink_row):
    seq_spec = lambda width: pl.BlockSpec((SEG, width), lambda b: (b, 0))
    return pl.pallas_call(
        _ctx_attn_kernel,
        grid=(N_PROMPT,),
        in_specs=[seq_spec(D), seq_spec(256), seq_spec(256), _const_spec((1, 128))],
        out_specs=seq_spec(D),
        out_shape=jax.ShapeDtypeStruct((N_PROMPT * SEG, D), BF16),
        compiler_params=_params(("arbitrary",)),
        name="ctx_attn",
    )(q, k, v, sink_row)


N_QBLK = 2048 // CHUNK
N_CTX = 512


def _rope(x, cos2, sin2):
    return x * cos2 + pltpu.roll(x, 64, 1) * sin2


def _lat_attn_kernel(q_ref, kp_ref, kc_ref, kn_ref, vp_ref, vc_ref, vn_ref, ck_ref, cv_ref,
                     cosq_ref, sinq_ref, cosp_ref, sinp_ref, cosn_ref, sinn_ref, sink_ref, o_ref):
    qi = pl.program_id(1)
    scale = 128 ** -0.5
    nk = 3 * CHUNK + N_CTX
    r = lax.broadcasted_iota(jnp.int32, (4 * CHUNK, nk), 0) & (CHUNK - 1)
    c = lax.broadcasted_iota(jnp.int32, (4 * CHUNK, nk), 1)
    far = 4 * CHUNK
    is_prev = c < CHUNK
    is_next = (c >= 2 * CHUNK) & (c < 3 * CHUNK)
    ok_prev = is_prev & (c >= r + jnp.where(qi > 0, 0, far))
    ok_next = is_next & (c - 2 * CHUNK <= r - jnp.where(qi < N_QBLK - 1, 0, far))
    mask = ok_prev | ok_next | jnp.logical_not(is_prev | is_next)
    for kh in range(2):
        ks = slice(128 * kh, 128 * kh + 128)
        k_all = jnp.concatenate([
            _rope(kp_ref[:, ks], cosp_ref[...], sinp_ref[...]).astype(BF16),
            _rope(kc_ref[:, ks], cosq_ref[...], sinq_ref[...]).astype(BF16),
            _rope(kn_ref[:, ks], cosn_ref[...], sinn_ref[...]).astype(BF16),
            ck_ref[0, :, ks].astype(BF16)], axis=0)
        v_all = jnp.concatenate([vp_ref[:, ks], vc_ref[:, ks], vn_ref[:, ks], cv_ref[0, :, ks]],
                                axis=0).astype(BF16)
        q4 = jnp.concatenate([
            (_rope(q_ref[:, 128 * (4 * kh + g):128 * (4 * kh + g) + 128].astype(F32),
                   cosq_ref[...], sinq_ref[...]) * scale).astype(BF16) for g in range(4)], axis=0)
        s = jnp.where(mask, _dot_nt(q4, k_all), -jnp.inf)
        o = _sink_softmax_pv(s, _sink_column(sink_ref, kh, CHUNK), v_all)
        for g in range(4):
            hd = 4 * kh + g
            o_ref[:, 128 * hd:128 * hd + 128] = o[g * CHUNK:(g + 1) * CHUNK].astype(BF16)


def _lat_attn(q, k, v, cache_k, cache_v, cos2, sin2, sink_row):
    base = N_PROMPT * SEG // CHUNK

    def blk(b, qi):
        return b * N_QBLK + qi

    def prev_blk(b, qi):
        return b * N_QBLK + jnp.maximum(qi - 1, 0)

    def next_blk(b, qi):
        return b * N_QBLK + jnp.minimum(qi + 1, N_QBLK - 1)

    kv_spec = lambda f: pl.BlockSpec((CHUNK, 256), lambda b, qi: (f(b, qi), 0))
    tab = lambda f: pl.BlockSpec((CHUNK, 128), lambda b, qi: (f(qi), 0))
    cache_spec = pl.BlockSpec((1, N_CTX, 256), lambda b, qi: (b, 0, 0))
    same = lambda qi: qi
    before = lambda qi: jnp.maximum(qi - 1, 0)
    after = lambda qi: jnp.minimum(qi + 1, N_QBLK - 1)
    return pl.pallas_call(
        _lat_attn_kernel,
        grid=(N_SAMPLE, N_QBLK),
        in_specs=[pl.BlockSpec((CHUNK, D), lambda b, qi: (base + blk(b, qi), 0)),
                  kv_spec(prev_blk), kv_spec(blk), kv_spec(next_blk),
                  kv_spec(prev_blk), kv_spec(blk), kv_spec(next_blk),
                  cache_spec, cache_spec,
                  tab(same), tab(same), tab(before), tab(before), tab(after), tab(after),
                  pl.BlockSpec((1, 128), lambda b, qi: (0, 0))],
        out_specs=pl.BlockSpec((CHUNK, D), lambda b, qi: (blk(b, qi), 0)),
        out_shape=jax.ShapeDtypeStruct((N_SAMPLE * 2048, D), BF16),
        compiler_params=_params(("arbitrary", "arbitrary")),
        name="lat_attn",
    )(q, k, k, k, v, v, v, cache_k, cache_v, cos2, sin2, cos2, sin2, cos2, sin2, sink_row)


def _rope_tables():
    length = 2048
    rows = length // 64
    rowp = jnp.repeat(jnp.arange(rows, dtype=F32), 64)
    colp = jnp.tile(jnp.arange(64, dtype=F32), rows)
    inv = 10000.0 ** (-jnp.arange(32, dtype=F32) / 32)
    ang = jnp.concatenate([rowp[:, None] * inv, colp[:, None] * inv], axis=-1)
    cos, sin = jnp.cos(ang), jnp.sin(ang)
    return jnp.concatenate([cos, cos], axis=-1), jnp.concatenate([-sin, sin], axis=-1)


def kernel(x_prompt, x_sample, state_l0_ssd, state_l0_dn, cache_l1_k, cache_l1_v, c, c_ctx, mod_w_l0, mod_b_l0, norm_mix_pre_l0, norm_mix_post_l0, norm_ffn_pre_l0, norm_ffn_post_l0, ffn_up_l0, ffn_conv_w_l0, ffn_conv_b_l0, ffn_down_l0, mod_w_l1, mod_b_l1, norm_mix_pre_l1, norm_mix_post_l1, norm_ffn_pre_l1, norm_ffn_post_l1, ffn_up_l1, ffn_conv_w_l1, ffn_conv_b_l1, ffn_down_l1, mix_in_l0, mix_out_l0, ssd_conv_w, ssd_conv_b, ssd_dt_bias, ssd_A_log, ssd_D, ssd_norm_w, dn_conv_w, dn_dt_bias, dn_A_log, dn_norm_w, mix_in_l1, mix_out_l1, sg_ln_w, sg_ln_b, sg_w_s, sg_b_s, attn_sink):
    row = lambda a: a.reshape(1, -1).astype(F32)
    x_p = x_prompt.reshape(N_PROMPT * SEG, D)
    x_s = x_sample.reshape(N_SAMPLE * 2048, D)
    cond8 = jnp.concatenate([c_ctx[None, :], c, jnp.zeros((5, D), F32)], axis=0)
    mod0 = _adaln(cond8, mod_w_l0, mod_b_l0)
    mod1 = _adaln(cond8, mod_w_l1, mod_b_l1)

    w = mix_in_l0
    w_main = jnp.concatenate([w[:, 1024:3072], w[:, 3104:6176], w[:, 0:1024], w[:, 6176:7200],
                              w[:, 3072:3104], w[:, 7200:7232], jnp.zeros((D, 64), F32)], axis=1).astype(BF16)
    cw = jnp.concatenate([ssd_conv_w, dn_conv_w], axis=1)
    cb = jnp.concatenate([ssd_conv_b, jnp.zeros((3072,), F32)]).reshape(1, -1)
    xbc, qkv, z, gate, small = _l0_in(x_p, x_s, mod0, row(norm_mix_pre_l0), w_main, cw, cb)
    bias_col = jnp.concatenate([ssd_dt_bias.reshape(-1), dn_dt_bias.reshape(-1), jnp.zeros((80,), F32)]).reshape(128, 1)
    mult_col = jnp.concatenate([-jnp.exp(ssd_A_log.reshape(-1)), -jnp.exp(dn_A_log.reshape(-1)),
                                jnp.zeros((80,), F32)]).reshape(128, 1)
    rowf, colf = _prep(small, bias_col, mult_col)
    y, new_ssd = _ssd(xbc, rowf, colf, state_l0_ssd, jnp.repeat(ssd_D, 64).reshape(1, D))
    o, new_dn = _dn(qkv, rowf, colf, state_l0_dn)
    x = _l0_out(y, z, o, gate, x_p, x_s, mod0, mix_out_l0.astype(BF16), row(ssd_norm_w), row(dn_norm_w),
                row(norm_mix_post_l0))
    x = _ffn(x, mod0, row(norm_ffn_pre_l0), ffn_up_l0.astype(BF16), ffn_conv_w_l0, row(ffn_conv_b_l0),
             ffn_down_l0.astype(BF16), row(norm_ffn_post_l0))

    bs_full = jnp.repeat(sg_b_s.T, 256, axis=1)
    mlp, q, k_p, k_s, v_p, v_s = _l1_in(x, mod1, row(norm_mix_pre_l1), mix_in_l1.astype(BF16), row(sg_ln_w),
                                        row(sg_ln_b), sg_w_s.astype(BF16), bs_full)
    sink_row = jnp.pad(attn_sink, (0, 120)).reshape(1, 128)
    attn_p = _ctx_attn(q, k_p, v_p, sink_row)
    cos2, sin2 = _rope_tables()
    attn_s = _lat_attn(q, k_s, v_s, cache_l1_k.reshape(N_SAMPLE, N_CTX, 256),
                       cache_l1_v.reshape(N_SAMPLE, N_CTX, 256), cos2, sin2, sink_row)
    x = _l1_out(mlp, attn_p, attn_s, x, mod1, mix_out_l1.astype(BF16), row(norm_mix_post_l1))
    y_p, y_s = _ffn(x, mod1, row(norm_ffn_pre_l1), ffn_up_l1.astype(BF16), ffn_conv_w_l1, row(ffn_conv_b_l1),
                    ffn_down_l1.astype(BF16), row(norm_ffn_post_l1), split_out=True)

    return (y_p.reshape(N_PROMPT, SEG, D),
            y_s.reshape(N_SAMPLE, 2048, D),
            new_ssd,
            new_dn,
            k_p.reshape(N_PROMPT, SEG, 2, 128),
            v_p.reshape(N_PROMPT, SEG, 2, 128))
```

```python
import functools

import jax
import jax.numpy as jnp
from jax import lax
from jax.experimental import pallas as pl
from jax.experimental.pallas import tpu as pltpu

F32 = jnp.float32
BF16 = jnp.bfloat16

D = 1024
N_PROMPT = 32
SEG = 256
CHUNK = 128
SEG_PER_SAMPLE = 8
N_SAMPLE = 2
NSEG = N_PROMPT + N_SAMPLE * SEG_PER_SAMPLE
T = NSEG * SEG
HALO = 8
EPS = 1e-6
VMEM_LIMIT = 56 * 1024 * 1024

SSD_HEADS = 16
DN_HEADS = 8
D_FF = 2816
FF_CHUNK = 256

CH_DT = 0
CH_BETA = 48
CH_ACS = 64
CH_GCS = 96


def _dot(a, b):
    return jnp.dot(a, b, preferred_element_type=F32)


def _dot_nt(a, b):
    return lax.dot_general(a, b, (((1,), (1,)), ((), ())), preferred_element_type=F32)


def _dot_tn(a, b):
    return lax.dot_general(a, b, (((0,), (0,)), ((), ())), preferred_element_type=F32)


def _silu(x):
    return x * jax.nn.sigmoid(x)


def _rms(x, w):
    return x * lax.rsqrt(jnp.mean(x * x, axis=-1, keepdims=True) + EPS) * w


TOK = 512
NTOK_PROMPT = N_PROMPT * SEG // TOK
TOK_PER_SAMPLE = 2048 // TOK
NTOK = T // TOK


def _mod_index(i):
    return jnp.where(i < NTOK_PROMPT, 0, 1 + jnp.maximum(i - NTOK_PROMPT, 0) // TOK_PER_SAMPLE)


def _halo_flags(i):
    r = lax.rem(jnp.maximum(i - NTOK_PROMPT, 0), TOK_PER_SAMPLE)
    in_sample = i >= NTOK_PROMPT
    has_prev = jnp.where(jnp.logical_and(in_sample, r != 0), 1.0, 0.0)
    has_next = jnp.where(jnp.logical_and(in_sample, r != TOK_PER_SAMPLE - 1), 1.0, 0.0)
    return has_prev, has_next


def _row_spec(width):
    return pl.BlockSpec((TOK, width), lambda i: (i, 0))


def _prev_spec(first_tile=0, n_rows=T):
    return pl.BlockSpec((HALO, D), lambda i: (
        jnp.clip((i - first_tile) * (TOK // HALO) - 1, 0, n_rows // HALO - 1), 0))


def _next_spec(first_tile=0, n_rows=T):
    return pl.BlockSpec((HALO, D), lambda i: (
        jnp.clip((i - first_tile + 1) * (TOK // HALO), 0, n_rows // HALO - 1), 0))


def _prompt_spec(width):
    return pl.BlockSpec((TOK, width), lambda i: (jnp.minimum(i, NTOK_PROMPT - 1), 0))


def _sample_spec(width):
    return pl.BlockSpec((TOK, width), lambda i: (jnp.maximum(i - NTOK_PROMPT, 0), 0))


def _mod_spec():
    return pl.BlockSpec((1, 8, D), lambda i: (_mod_index(i), 0, 0))


def _const_spec(shape):
    nd = len(shape)
    return pl.BlockSpec(shape, lambda *_: (0,) * nd, pipeline_mode=pl.Buffered(1))


def _params(sem):
    return pltpu.CompilerParams(dimension_semantics=sem, vmem_limit_bytes=VMEM_LIMIT)


def _adaln_kernel(c_ref, w_ref, b_ref, o_ref):
    s = _silu(c_ref[...]).astype(BF16)
    o_ref[...] = _dot(s, w_ref[...].astype(BF16)) + b_ref[...]


def _adaln(cond8, w, b):
    tn = 768
    out = pl.pallas_call(
        _adaln_kernel,
        grid=(6 * D // tn,),
        in_specs=[pl.BlockSpec((8, D), lambda j: (0, 0)),
                  pl.BlockSpec((D, tn), lambda j: (0, j)),
                  pl.BlockSpec((1, tn), lambda j: (0, j))],
        out_specs=pl.BlockSpec((8, tn), lambda j: (0, j)),
        out_shape=jax.ShapeDtypeStruct((8, 6 * D), F32),
        compiler_params=_params(("arbitrary",)),
        name="adaln",
    )(cond8, w, b.reshape(1, -1))
    mod = out.reshape(8, 6, D)[:3]
    return jnp.pad(mod, ((0, 0), (0, 2), (0, 0)))


L0_CONV = 5120
L0_COLS = 7296
CONV_CHUNK = 512


def _pick_rows(i, prompt_ref, sample_ref):
    return jnp.where(i < NTOK_PROMPT, prompt_ref[...], sample_ref[...])


def _halo_rows(i, x, xp_ref, xn_ref):
    xe = jnp.concatenate([x, xn_ref[...], xp_ref[...]], axis=0)
    has_prev, has_next = _halo_flags(i)
    r = lax.broadcasted_iota(jnp.int32, (2 * HALO, 1), 0)
    return xe, (jnp.where(r < HALO, has_next, has_prev), jnp.where(i < NTOK_PROMPT, 1.0, 0.0))


def _dwconv(p, tile_info, w_ref, b_ref, cs, width):
    valid, two_seq = tile_info
    n = p.shape[0]
    p = jnp.concatenate([p[0:TOK], p[TOK:n] * valid], axis=0)
    half = width // 2
    taps = [p if k == half else pltpu.roll(p, (half - k) % n, 0) for k in range(width)]
    acc = b_ref[:, cs]
    for k in range(width):
        acc = acc + taps[k][0:TOK] * w_ref[k:k + 1, cs]
    lo, hi = SEG - 8, SEG + 8
    j = lax.broadcasted_iota(jnp.int32, (16, 1), 0)
    fix = b_ref[:, cs]
    for k in range(width):
        src = j + (k - half)
        crosses = jnp.where(j < 8, jnp.where(src >= 8, 1.0, 0.0), jnp.where(src < 8, 1.0, 0.0))
        keep = 1.0 - two_seq * crosses
        fix = fix + taps[k][lo:hi] * keep * w_ref[k:k + 1, cs]
    return jnp.concatenate([acc[0:lo], fix, acc[hi:TOK]], axis=0)


def _l0_in_kernel(zero_ref, xpr_ref, xsa_ref, xp_ref, xn_ref, m_ref, nw_ref, w_ref, cw_ref, cb_ref,
                  xbc_ref, qkv_ref, z_ref, gate_ref, small_ref, p_s):
    i = pl.program_id(0)
    shift = m_ref[0, 0:1, :]
    scale = m_ref[0, 1:2, :]
    xe, valid = _halo_rows(i, _pick_rows(i, xpr_ref, xsa_ref), xp_ref, xn_ref)
    he = (_rms(xe, nw_ref[...]) * (1.0 + scale) + shift).astype(BF16)
    hc = he[0:TOK]

    def plain(j):
        if j < 4:
            z_ref[:, 256 * j:256 * j + 256] = _dot(hc, w_ref[:, 5120 + 256 * j:5376 + 256 * j]).astype(BF16)
        elif j < 8:
            j -= 4
            gate_ref[:, 256 * j:256 * j + 256] = _dot(hc, w_ref[:, 6144 + 256 * j:6400 + 256 * j]).astype(BF16)
        elif j == 8:
            small_ref[...] = _dot(hc, w_ref[:, 7168:7296])

    z = zero_ref[0]
    n_conv = L0_CONV // CONV_CHUNK
    conv_cols = lambda cc: slice(cc * CONV_CHUNK, (cc + 1) * CONV_CHUNK)
    p_s[z] = _dot(he, w_ref[:, conv_cols(0)])
    for cc in range(n_conv):
        cur = p_s[cc % 2 + z]
        if cc + 1 < n_conv:
            p_s[(cc + 1) % 2 + z] = _dot(he, w_ref[:, conv_cols(cc + 1)])
        plain(cc)
        cs = conv_cols(cc)
        val = _silu(_dwconv(cur, valid, cw_ref, cb_ref, cs, 5)).astype(BF16)
        if cc < 4:
            xbc_ref[:, cs] = val
        else:
            for t in range(4):
                qkv_ref[(cc - 4) * 4 + t] = val[:, t * 128:(t + 1) * 128]


def _l0_in(x_prompt, x_sample, modt, nw, w_main, cw, cb):
    n_s = N_SAMPLE * 2048
    return pl.pallas_call(
        _l0_in_kernel,
        grid=(NTOK,),
        in_specs=[pl.BlockSpec(memory_space=pltpu.SMEM), _prompt_spec(D), _sample_spec(D),
                  _prev_spec(NTOK_PROMPT, n_s), _next_spec(NTOK_PROMPT, n_s), _mod_spec(),
                  _const_spec((1, D)), _const_spec((D, L0_COLS)),
                  _const_spec((5, L0_CONV)), _const_spec((1, L0_CONV))],
        out_specs=[_row_spec(2048),
                   pl.BlockSpec((24, TOK, 128), lambda i: (0, i, 0)),
                   _row_spec(D), _row_spec(D), _row_spec(128)],
        out_shape=[jax.ShapeDtypeStruct((T, 2048), BF16),
                   jax.ShapeDtypeStruct((24, T, 128), BF16),
                   jax.ShapeDtypeStruct((T, D), BF16),
                   jax.ShapeDtypeStruct((T, D), BF16),
                   jax.ShapeDtypeStruct((T, 128), F32)],
        scratch_shapes=[pltpu.VMEM((2, TOK + 2 * HALO, CONV_CHUNK), F32)],
        compiler_params=_params(("arbitrary",)),
        name="l0_in",
    )(jnp.zeros((1,), jnp.int32), x_prompt, x_sample, x_sample, x_sample, modt, nw, w_main, cw, cb)


def _split3_dot(a, tri):
    a1 = a.astype(BF16)
    r1 = a - a1.astype(F32)
    a2 = r1.astype(BF16)
    a3 = (r1 - a2.astype(F32)).astype(BF16)
    return _dot(a1, tri) + _dot(a2, tri) + _dot(a3, tri)


def _prep_kernel(s_ref, bias_ref, mult_ref, rowf_ref, colf_ref):
    row = lax.broadcasted_iota(jnp.int32, (CHUNK, CHUNK), 0)
    col = lax.broadcasted_iota(jnp.int32, (CHUNK, CHUNK), 1)
    upper = jnp.where(row <= col, 1.0, 0.0).astype(BF16)
    lower = jnp.where(row >= col, 1.0, 0.0).astype(BF16)
    backward = ((row >= 16) & (row < 32)) | ((row >= 40) & (row < 48))
    for c in range(TOK // CHUNK):
        rs = slice(c * CHUNK, (c + 1) * CHUNK)
        raw_t = s_ref[rs, :].T
        xb = raw_t + bias_ref[...]
        sp = jnp.maximum(xb, 0.0) + jnp.log1p(jnp.exp(-jnp.abs(xb)))
        vals = jnp.where(row < 48, sp, jax.nn.sigmoid(raw_t))
        a = jnp.where(row < 48, sp * mult_ref[...], 0.0)
        cum = jnp.where(backward, _split3_dot(a, lower), _split3_dot(a, upper))
        out = jnp.concatenate([vals[0:64], cum[0:64]], axis=0)
        rowf_ref[c] = out
        colf_ref[rs, :] = out.T


def _prep(small, bias_col, mult_col):
    return pl.pallas_call(
        _prep_kernel,
        grid=(NTOK,),
        in_specs=[_row_spec(128), _const_spec((128, 1)), _const_spec((128, 1))],
        out_specs=[pl.BlockSpec((TOK // CHUNK, CHUNK, CHUNK), lambda i: (i, 0, 0)), _row_spec(128)],
        out_shape=[jax.ShapeDtypeStruct((T // CHUNK, CHUNK, CHUNK), F32),
                   jax.ShapeDtypeStruct((T, 128), F32)],
        compiler_params=_params(("arbitrary",)),
        name="scan_prep",
    )(small, bias_col, mult_col)


def _scan_seg(d, j):
    return j + d * (NSEG - 1 - 2 * j)


def _scan_batch(seg):
    return jnp.maximum(seg - N_PROMPT, 0) // SEG_PER_SAMPLE


def _scan_specs(heads, rows):
    seg_map = lambda d, j: _scan_seg(d, j)
    return dict(
        rowf=pl.BlockSpec((2, CHUNK, CHUNK), lambda d, j: (seg_map(d, j), 0, 0)),
        colf=pl.BlockSpec((SEG, 128), lambda d, j: (seg_map(d, j), 0)),
        h0=pl.BlockSpec((1, 1, heads, rows, 128), lambda d, j: (_scan_batch(seg_map(d, j)), d, 0, 0, 0)),
        ns=pl.BlockSpec((1, 1, heads, rows, 128),
                        lambda d, j: (jnp.minimum(seg_map(d, j), N_PROMPT - 1), d, 0, 0, 0)),
    )


def _scan_init(seg, direction, st, h0_ref):
    is_prompt = seg < N_PROMPT
    r = lax.rem(jnp.maximum(seg - N_PROMPT, 0), SEG_PER_SAMPLE)
    first = 0 if direction == 0 else SEG_PER_SAMPLE - 1

    @pl.when(is_prompt)
    def _():
        st[...] = jnp.zeros_like(st)

    @pl.when(jnp.logical_and(jnp.logical_not(is_prompt), r == first))
    def _():
        st[...] = h0_ref[0, 0].reshape(st.shape)

    return is_prompt


def _ssd_body(direction, seg, xbc_ref, rowf_ref, colf_ref, h0_ref, drow_ref, y_ref, ns_ref,
              st, m_s, e_s, w_s, gl_s):
    is_prompt = _scan_init(seg, direction, st, h0_ref)
    row = lax.broadcasted_iota(jnp.int32, (CHUNK, CHUNK), 0)
    col = lax.broadcasted_iota(jnp.int32, (CHUNK, CHUNK), 1)
    mask = (col <= row) if direction == 0 else (col >= row)
    lane = lax.broadcasted_iota(jnp.int32, (CHUNK, 256), 1)
    heads = [(lane >= 64 * hh) & (lane < 64 * hh + 64) for hh in range(4)]
    n_chunk = SEG // CHUNK
    order = range(n_chunk) if direction == 0 else range(n_chunk - 1, -1, -1)

    for c in range(n_chunk):
        rs = slice(c * CHUNK, (c + 1) * CHUNK)
        for g in range(4):
            bg = xbc_ref[rs, 1024 + 128 * g:1152 + 128 * g]
            cg = xbc_ref[rs, 1536 + 128 * g:1664 + 128 * g]
            gmat = _dot_nt(cg, bg)
            e_in = jnp.zeros((CHUNK, 256), F32)
            w_out = jnp.zeros((CHUNK, 256), F32)
            for hh in range(4):
                h = 4 * g + hh
                ch_dt = CH_DT + 16 * direction + h
                ch_ac = CH_ACS + 16 * direction + h
                ar = rowf_ref[c, ch_ac:ch_ac + 1, :]
                dt_r = rowf_ref[c, ch_dt:ch_dt + 1, :]
                last = ar[:, CHUNK - 1:CHUNK] if direction == 0 else ar[:, 0:1]
                ac_b = jnp.broadcast_to(colf_ref[rs, ch_ac:ch_ac + 1], (CHUNK, CHUNK))
                dt_b = jnp.broadcast_to(colf_ref[rs, ch_dt:ch_dt + 1], (CHUNK, CHUNK))
                lmat = jnp.exp(jnp.where(mask, ac_b - ar, -jnp.inf))
                m_s[16 * c + h] = (gmat * lmat * dt_r).astype(BF16)
                e_b = jnp.exp(ac_b)
                w_b = dt_b * jnp.exp(last - ac_b)
                e_in = jnp.where(heads[hh], jnp.concatenate([e_b, e_b], axis=1), e_in)
                w_out = jnp.where(heads[hh], jnp.concatenate([w_b, w_b], axis=1), w_out)
                gl_s[16 * c + h] = jnp.broadcast_to(jnp.exp(last), (8, CHUNK))
            e_s[4 * c + g] = e_in
            w_s[4 * c + g] = w_out

    for c in order:
        rs = slice(c * CHUNK, (c + 1) * CHUNK)
        for g in range(4):
            bg = xbc_ref[rs, 1024 + 128 * g:1152 + 128 * g]
            cg = xbc_ref[rs, 1536 + 128 * g:1664 + 128 * g]
            xf = xbc_ref[rs, 256 * g:256 * g + 256].astype(F32)
            y = _dot_nt(cg, st[256 * g:256 * g + 256, :].astype(BF16)) * e_s[4 * c + g]
            for hh in range(4):
                y = y + _dot(m_s[16 * c + 4 * g + hh], jnp.where(heads[hh], xf, 0.0).astype(BF16))
            if direction == 0:
                y = y + xf * drow_ref[:, 256 * g:256 * g + 256]
            y_ref[0, rs, 256 * g:256 * g + 256] = y.astype(BF16)
            upd = _dot_tn((xf * w_s[4 * c + g]).astype(BF16), bg)
            for hh in range(4):
                hr = slice(256 * g + 64 * hh, 256 * g + 64 * hh + 64)
                st[hr, :] = st[hr, :] * gl_s[16 * c + 4 * g + hh, 0:1, :] + upd[64 * hh:64 * hh + 64]

    @pl.when(is_prompt)
    def _():
        ns_ref[0, 0] = st[...].reshape(SSD_HEADS, 64, 128)


def _ssd_kernel(xbc_ref, rowf_ref, colf_ref, h0_ref, drow_ref, y_ref, ns_ref, *scratch):
    d = pl.program_id(0)
    seg = _scan_seg(d, pl.program_id(1))
    for direction in (0, 1):
        pl.when(d == direction)(functools.partial(
            _ssd_body, direction, seg, xbc_ref, rowf_ref, colf_ref, h0_ref, drow_ref, y_ref, ns_ref, *scratch))


def _ssd(xbc, rowf, colf, h0, drow):
    sp = _scan_specs(SSD_HEADS, 64)
    n_chunk = SEG // CHUNK
    return pl.pallas_call(
        _ssd_kernel,
        grid=(2, NSEG),
        in_specs=[pl.BlockSpec((SEG, 2048), lambda d, j: (_scan_seg(d, j), 0)),
                  sp["rowf"], sp["colf"], sp["h0"],
                  pl.BlockSpec((1, D), lambda d, j: (0, 0))],
        out_specs=[pl.BlockSpec((1, SEG, D), lambda d, j: (d, _scan_seg(d, j), 0)), sp["ns"]],
        out_shape=[jax.ShapeDtypeStruct((2, T, D), BF16),
                   jax.ShapeDtypeStruct((N_PROMPT, 2, SSD_HEADS, 64, 128), F32)],
        scratch_shapes=[pltpu.VMEM((1024, 128), F32),
                        pltpu.VMEM((n_chunk * SSD_HEADS, CHUNK, CHUNK), BF16),
                        pltpu.VMEM((n_chunk * 4, CHUNK, 256), F32),
                        pltpu.VMEM((n_chunk * 4, CHUNK, 256), F32),
                        pltpu.VMEM((n_chunk * SSD_HEADS, 8, CHUNK), F32)],
        compiler_params=_params(("arbitrary", "arbitrary")),
        name="ssd_scan",
    )(xbc, rowf, colf, h0, drow)


def _l2n(x):
    return x * lax.rsqrt(jnp.sum(x * x, axis=-1, keepdims=True) + EPS)


def _dn_body(direction, seg, qkv_ref, rowf_ref, colf_ref, s0_ref, lvl_ref, o_ref, ns_ref,
             st, a_s, d_s, rhs_s, qk_s, qd_s, ke_s, u_s, w_s, gl_s):
    is_prompt = _scan_init(seg, direction, st, s0_ref)
    row = lax.broadcasted_iota(jnp.int32, (CHUNK, CHUNK), 0)
    col = lax.broadcasted_iota(jnp.int32, (CHUNK, CHUNK), 1)
    tri = (col <= row) if direction == 0 else (col >= row)
    strict = (col < row) if direction == 0 else (col > row)
    eye = jnp.where(row == col, 1.0, 0.0)
    n_chunk = SEG // CHUNK
    order = range(n_chunk) if direction == 0 else range(n_chunk - 1, -1, -1)

    def operands(h, c):
        ch_b = CH_BETA + 8 * direction + h
        ch_g = CH_GCS + 8 * direction + h
        p = n_chunk * h + c
        rs = slice(c * CHUNK, (c + 1) * CHUNK)
        qn = _l2n(qkv_ref[h, rs, :].astype(F32)) * (128 ** -0.5)
        kn = _l2n(qkv_ref[DN_HEADS + h, rs, :].astype(F32))
        v = qkv_ref[2 * DN_HEADS + h, rs, :].astype(F32)
        beta = colf_ref[rs, ch_b:ch_b + 1]
        gc = colf_ref[rs, ch_g:ch_g + 1]
        gr = rowf_ref[c, ch_g:ch_g + 1, :]
        glast = gr[:, CHUNK - 1:CHUNK] if direction == 0 else gr[:, 0:1]
        decay = jnp.exp(jnp.where(tri, gc - gr, -jnp.inf))
        kb = kn * beta
        kq = _dot_nt(jnp.concatenate([kb, qn], axis=0).astype(BF16), kn.astype(BF16))
        a = jnp.where(strict, kq[0:CHUNK] * decay, 0.0)
        a_s[p] = a.astype(BF16)
        d_s[p] = eye - a * lvl_ref[0].astype(F32)
        qk_s[p] = (kq[CHUNK:2 * CHUNK] * decay).astype(BF16)
        eg = jnp.exp(gc)
        rhs_s[p] = jnp.concatenate([v * beta, kb * eg], axis=1).astype(BF16)
        qd_s[p] = (qn * eg).astype(BF16)
        ke_s[p] = (kn * jnp.exp(glast - gc)).astype(BF16)
        gl_s[p] = jnp.broadcast_to(jnp.exp(glast), (8, CHUNK))

    def double(lvl, p):
        tm = d_s[p]
        tb = tm.astype(BF16)
        a_off = a_s[p] * lvl_ref[lvl]
        d_s[p] = tm - _dot(tb, _dot(a_off, tb).astype(BF16))

    for h in range(DN_HEADS):
        for c in range(n_chunk):
            operands(h, c)
    for lvl in range(1, 7):
        for p in range(n_chunk * DN_HEADS):
            double(lvl, p)

    for p in range(n_chunk * DN_HEADS):
        uw = _dot(d_s[p].astype(BF16), rhs_s[p])
        u_s[p] = uw[:, 0:CHUNK]
        w_s[p] = uw[:, CHUNK:2 * CHUNK].astype(BF16)

    for c in order:
        wq = [_dot(jnp.concatenate([w_s[n_chunk * h + c], qd_s[n_chunk * h + c]], axis=0),
                   st[h].astype(BF16)) for h in range(DN_HEADS)]
        for h in range(DN_HEADS):
            p = n_chunk * h + c
            vb = (u_s[p] - wq[h][0:CHUNK]).astype(BF16)
            o = wq[h][CHUNK:2 * CHUNK] + _dot(qk_s[p], vb)
            st[h] = st[h] * gl_s[p, 0:1, :] + _dot_tn(ke_s[p], vb)
            o_ref[0, h, c * CHUNK:(c + 1) * CHUNK, :] = o.astype(BF16)

    @pl.when(is_prompt)
    def _():
        ns_ref[0, 0] = st[...]


def _dn_kernel(qkv_ref, rowf_ref, colf_ref, s0_ref, lvl_ref, o_ref, ns_ref, *scratch):
    d = pl.program_id(0)
    seg = _scan_seg(d, pl.program_id(1))
    for direction in (0, 1):
        pl.when(d == direction)(functools.partial(
            _dn_body, direction, seg, qkv_ref, rowf_ref, colf_ref, s0_ref, lvl_ref, o_ref, ns_ref, *scratch))


def _sibling_masks():
    i = jnp.arange(CHUNK)[:, None]
    j = jnp.arange(CHUNK)[None, :]
    lv = [((i >> (l + 1)) == (j >> (l + 1))) & ((i >> l) != (j >> l)) for l in range(7)]
    return jnp.stack(lv).astype(BF16)


def _dn(qkv, rowf, colf, s0):
    sp = _scan_specs(DN_HEADS, CHUNK)
    n_prob = DN_HEADS * SEG // CHUNK
    mat = lambda dt: pltpu.VMEM((n_prob, CHUNK, CHUNK), dt)
    return pl.pallas_call(
        _dn_kernel,
        grid=(2, NSEG),
        in_specs=[pl.BlockSpec((24, SEG, 128), lambda d, j: (0, _scan_seg(d, j), 0)),
                  sp["rowf"], sp["colf"], sp["h0"], _const_spec((7, CHUNK, CHUNK))],
        out_specs=[pl.BlockSpec((1, DN_HEADS, SEG, 128), lambda d, j: (d, 0, _scan_seg(d, j), 0)),
                   sp["ns"]],
        out_shape=[jax.ShapeDtypeStruct((2, DN_HEADS, T, 128), BF16),
                   jax.ShapeDtypeStruct((N_PROMPT, 2, DN_HEADS, CHUNK, CHUNK), F32)],
        scratch_shapes=[pltpu.VMEM((DN_HEADS, CHUNK, CHUNK), F32),
                        mat(BF16), mat(F32),
                        pltpu.VMEM((n_prob, CHUNK, 2 * CHUNK), BF16),
                        mat(BF16), mat(BF16), mat(BF16),
                        mat(F32), mat(BF16),
                        pltpu.VMEM((n_prob, 8, CHUNK), F32)],
        compiler_params=_params(("arbitrary", "arbitrary")),
        name="dn_scan",
    )(qkv, rowf, colf, s0, _sibling_masks())


def _residual(x, m_ref, gate_row, out, pw_ref):
    gate = m_ref[0, gate_row:gate_row + 1, :]
    return x + gate * _rms(out, pw_ref[...])


def _l0_out_kernel(y_ref, z_ref, o_ref, gate_ref, xpr_ref, xsa_ref, m_ref, w_ref, snw_ref, dnw_ref, pw_ref,
                   xo_ref):
    y = (y_ref[0].astype(F32) + y_ref[1].astype(F32)) * _silu(z_ref[...].astype(F32))
    yn = _rms(y, snw_ref[...]).astype(BF16)
    heads = []
    for h in range(DN_HEADS):
        oh = o_ref[0, h].astype(F32) + o_ref[1, h].astype(F32)
        gh = gate_ref[:, 128 * h:128 * h + 128].astype(F32)
        heads.append((_rms(oh, dnw_ref[...]) * _silu(gh)).astype(BF16))
    on = jnp.concatenate(heads, axis=1)
    out = _dot(yn, w_ref[0:1024, :]) + _dot(on, w_ref[1024:2048, :])
    x = _pick_rows(pl.program_id(0), xpr_ref, xsa_ref)
    xo_ref[...] = _residual(x, m_ref, 2, out, pw_ref)


def _l0_out(y, z, o, gate, x_prompt, x_sample, modt, w_out, snw, dnw, pw):
    return pl.pallas_call(
        _l0_out_kernel,
        grid=(NTOK,),
        in_specs=[pl.BlockSpec((2, TOK, D), lambda i: (0, i, 0)), _row_spec(D),
                  pl.BlockSpec((2, DN_HEADS, TOK, 128), lambda i: (0, 0, i, 0)), _row_spec(D),
                  _prompt_spec(D), _sample_spec(D), _mod_spec(), _const_spec((2048, D)),
                  _const_spec((1, D)), _const_spec((1, 128)), _const_spec((1, D))],
        out_specs=_row_spec(D),
        out_shape=jax.ShapeDtypeStruct((T, D), F32),
        compiler_params=_params(("arbitrary",)),
        name="l0_out",
    )(y, z, o, gate, x_prompt, x_sample, modt, w_out, snw, dnw, pw)


def _l1_out_kernel(a_ref, bp_ref, bs_ref, x_ref, m_ref, w_ref, pw_ref, xo_ref):
    def finish(b_ref):
        out = _dot(a_ref[...], w_ref[0:1024, :]) + _dot(b_ref[...], w_ref[1024:2048, :])
        xo_ref[...] = _residual(x_ref[...], m_ref, 2, out, pw_ref)

    is_prompt = pl.program_id(0) < NTOK_PROMPT
    pl.when(is_prompt)(functools.partial(finish, bp_ref))
    pl.when(jnp.logical_not(is_prompt))(functools.partial(finish, bs_ref))


def _l1_out(a, b_prompt, b_sample, x, modt, w_out, pw):
    return pl.pallas_call(
        _l1_out_kernel,
        grid=(NTOK,),
        in_specs=[_row_spec(D), _prompt_spec(D), _sample_spec(D),
                  _row_spec(D), _mod_spec(), _const_spec((2048, D)), _const_spec((1, D))],
        out_specs=_row_spec(D),
        out_shape=jax.ShapeDtypeStruct((T, D), F32),
        compiler_params=_params(("arbitrary",)),
        name="l1_out",
    )(a, b_prompt, b_sample, x, modt, w_out, pw)


def _store_split(i, val, prompt_ref, sample_ref):
    @pl.when(i < NTOK_PROMPT)
    def _():
        prompt_ref[...] = val

    @pl.when(i >= NTOK_PROMPT)
    def _():
        sample_ref[...] = val


def _ffn_kernel(zero_ref, x_ref, xp_ref, xn_ref, m_ref, nw_ref, wu_ref, cw_ref, cb_ref, wd_ref, pw_ref,
                *out_and_scratch):
    *out_refs, p_s = out_and_scratch
    i = pl.program_id(0)
    z = zero_ref[0]
    shift = m_ref[0, 3:4, :]
    scale = m_ref[0, 4:5, :]
    xe, valid = _halo_rows(i, x_ref[...], xp_ref, xn_ref)
    he = (_rms(xe, nw_ref[...]) * (1.0 + scale) + shift).astype(BF16)
    n_chunk = D_FF // FF_CHUNK
    cols = lambda cc: [slice(base + cc * FF_CHUNK, base + (cc + 1) * FF_CHUNK) for base in (0, D_FF)]

    def up(cc):
        for half, cs in enumerate(cols(cc)):
            p_s[cc % 2 + z, half] = _dot(he, wu_ref[:, cs])

    acc = jnp.zeros((TOK, D), F32)
    up(0)
    for cc in range(n_chunk):
        cur = [p_s[cc % 2 + z, half] for half in range(2)]
        if cc + 1 < n_chunk:
            up(cc + 1)
        a, b = [_dwconv(p, valid, cw_ref, cb_ref, cs, 3) for p, cs in zip(cur, cols(cc))]
        g = (_silu(a) * b).astype(BF16)
        acc = acc + _dot(g, wd_ref[cc * FF_CHUNK:(cc + 1) * FF_CHUNK, :])
    xo = _residual(x_ref[...], m_ref, 5, acc, pw_ref)
    if len(out_refs) == 1:
        out_refs[0][...] = xo
    else:
        _store_split(i, xo, *out_refs)


def _ffn(x, modt, nw, wu, cw, cb, wd, pw, split_out=False):
    if split_out:
        out_specs = [_prompt_spec(D), _sample_spec(D)]
        out_shape = [jax.ShapeDtypeStruct((N_PROMPT * SEG, D), F32),
                     jax.ShapeDtypeStruct((N_SAMPLE * 2048, D), F32)]
    else:
        out_specs = _row_spec(D)
        out_shape = jax.ShapeDtypeStruct((T, D), F32)
    return pl.pallas_call(
        _ffn_kernel,
        grid=(NTOK,),
        in_specs=[pl.BlockSpec(memory_space=pltpu.SMEM),
                  _row_spec(D), _prev_spec(), _next_spec(), _mod_spec(), _const_spec((1, D)),
                  _const_spec((D, 2 * D_FF)), _const_spec((3, 2 * D_FF)), _const_spec((1, 2 * D_FF)),
                  _const_spec((D_FF, D)), _const_spec((1, D))],
        out_specs=out_specs,
        out_shape=out_shape,
        scratch_shapes=[pltpu.VMEM((2, 2, TOK + 2 * HALO, FF_CHUNK), F32)],
        compiler_params=_params(("arbitrary",)),
        name="conv_ffn",
    )(jnp.zeros((1,), jnp.int32), x, x, x, modt, nw, wu, cw, cb, wd, pw)


def _l1_in_kernel(x_ref, m_ref, nw_ref, w_ref, lnw_ref, lnb_ref, ws_ref, bs_ref,
                  mlp_ref, qp_ref, qs_ref, kp_ref, ks_ref, vp_ref, vs_ref):
    i = pl.program_id(0)
    shift = m_ref[0, 0:1, :]
    scale = m_ref[0, 1:2, :]
    h = (_rms(x_ref[...], nw_ref[...]) * (1.0 + scale) + shift).astype(BF16)
    gv = _dot(h, w_ref[:, 1024:2048])
    q = _dot(h, w_ref[:, 2048:3072]).astype(BF16)
    k = _dot(h, w_ref[:, 3072:3328])
    v = _dot(h, w_ref[:, 3328:3584])
    u = _dot(h, w_ref[:, 0:1024])
    gv = jax.nn.gelu(gv)
    gc = gv - jnp.mean(gv, axis=-1, keepdims=True)
    gn = gc * lax.rsqrt(jnp.mean(gc * gc, axis=-1, keepdims=True) + EPS) * lnw_ref[...] + lnb_ref[...]
    gb = gn.astype(BF16)
    u = jax.nn.gelu(u)
    for c in range(TOK // CHUNK):
        rs = slice(c * CHUNK, (c + 1) * CHUNK)
        for g in range(4):
            cs = slice(256 * g, 256 * g + 256)
            sv = _dot(ws_ref[g], gb[rs, cs]) + bs_ref[:, cs]
            mlp_ref[rs, cs] = (u[rs, cs] * sv).astype(BF16)
    _store_split(i, q, qp_ref, qs_ref)
    _store_split(i, k, kp_ref, ks_ref)
    _store_split(i, v, vp_ref, vs_ref)


def _l1_in(x, modt, nw, w_in, lnw, lnb, ws, bs):
    n_p, n_s = N_PROMPT * SEG, N_SAMPLE * 2048
    return pl.pallas_call(
        _l1_in_kernel,
        grid=(NTOK,),
        in_specs=[_row_spec(D), _mod_spec(), _const_spec((1, D)), _const_spec((D, 3584)),
                  _const_spec((1, D)), _const_spec((1, D)), _const_spec((4, CHUNK, CHUNK)),
                  _const_spec((CHUNK, D))],
        out_specs=[_row_spec(D), _prompt_spec(D), _sample_spec(D), _prompt_spec(256), _sample_spec(256),
                   _prompt_spec(256), _sample_spec(256)],
        out_shape=[jax.ShapeDtypeStruct((T, D), BF16),
                   jax.ShapeDtypeStruct((n_p, D), BF16), jax.ShapeDtypeStruct((n_s, D), BF16),
                   jax.ShapeDtypeStruct((n_p, 256), F32), jax.ShapeDtypeStruct((n_s, 256), F32),
                   jax.ShapeDtypeStruct((n_p, 256), F32), jax.ShapeDtypeStruct((n_s, 256), F32)],
        compiler_params=_params(("arbitrary",)),
        name="l1_in",
    )(x, modt, nw, w_in, lnw, lnb, ws, bs)


def _sink_softmax_pv(s, sink_col, vb):
    m = jnp.maximum(jnp.max(s, axis=-1, keepdims=True), sink_col)
    p = jnp.exp(s - m)
    den = jnp.sum(p, axis=-1, keepdims=True) + jnp.exp(sink_col - m)
    return _dot(p.astype(BF16), vb) / den


def _sink_column(sink_ref, kh, rows):
    parts = [jnp.broadcast_to(sink_ref[:, 4 * kh + g:4 * kh + g + 1], (rows, 1)) for g in range(4)]
    return jnp.concatenate(parts, axis=0)


CTX_SEQS = 4


def _ctx_attn_kernel(q_ref, k_ref, v_ref, sink_ref, o_ref):
    scale = 128 ** -0.5
    for b in range(CTX_SEQS):
        rows = slice(b * SEG, (b + 1) * SEG)
        for kh in range(2):
            kb = k_ref[rows, 128 * kh:128 * kh + 128].astype(BF16)
            vb = v_ref[rows, 128 * kh:128 * kh + 128].astype(BF16)
            q4 = jnp.concatenate([q_ref[rows, 128 * (4 * kh + g):128 * (4 * kh + g) + 128] for g in range(4)],
                                 axis=0)
            s = _dot_nt(q4, kb) * scale
            o = _sink_softmax_pv(s, _sink_column(sink_ref, kh, SEG), vb)
            for g in range(4):
                hd = 4 * kh + g
                o_ref[rows, 128 * hd:128 * hd + 128] = o[g * SEG:(g + 1) * SEG].astype(BF16)


def _ctx_attn(q, k, v, sink_row):
    seq_spec = lambda width: pl.BlockSpec((CTX_SEQS * SEG, width), lambda b: (b, 0))
    return pl.pallas_call(
        _ctx_attn_kernel,
        grid=(N_PROMPT // CTX_SEQS,),
        in_specs=[seq_spec(D), seq_spec(256), seq_spec(256), _const_spec((1, 128))],
        out_specs=seq_spec(D),
        out_shape=jax.ShapeDtypeStruct((N_PROMPT * SEG, D), BF16),
        compiler_params=_params(("arbitrary",)),
        name="ctx_attn",
    )(q, k, v, sink_row)


N_QBLK = 2048 // CHUNK
N_CTX = 512


def _rope(x, cos2, sin2):
    return x * cos2 + pltpu.roll(x, 64, 1) * sin2


def _lat_attn_kernel(q_ref, kp_ref, kc_ref, kn_ref, vp_ref, vc_ref, vn_ref, ck_ref, cv_ref,
                     cosq_ref, sinq_ref, cosp_ref, sinp_ref, cosn_ref, sinn_ref, sink_ref, o_ref):
    qi = pl.program_id(0)
    scale = 128 ** -0.5
    nk = 3 * CHUNK + N_CTX
    r = lax.broadcasted_iota(jnp.int32, (4 * CHUNK, nk), 0) & (CHUNK - 1)
    c = lax.broadcasted_iota(jnp.int32, (4 * CHUNK, nk), 1)
    far = 4 * CHUNK
    is_prev = c < CHUNK
    is_next = (c >= 2 * CHUNK) & (c < 3 * CHUNK)
    ok_prev = is_prev & (c >= r + jnp.where(qi > 0, 0, far))
    ok_next = is_next & (c - 2 * CHUNK <= r - jnp.where(qi < N_QBLK - 1, 0, far))
    mask = ok_prev | ok_next | jnp.logical_not(is_prev | is_next)
    for b in range(N_SAMPLE):
        for kh in range(2):
            ks = slice(128 * kh, 128 * kh + 128)
            k_all = jnp.concatenate([
                _rope(kp_ref[b, :, ks], cosp_ref[...], sinp_ref[...]).astype(BF16),
                _rope(kc_ref[b, :, ks], cosq_ref[...], sinq_ref[...]).astype(BF16),
                _rope(kn_ref[b, :, ks], cosn_ref[...], sinn_ref[...]).astype(BF16),
                ck_ref[b, :, ks].astype(BF16)], axis=0)
            v_all = jnp.concatenate([vp_ref[b, :, ks], vc_ref[b, :, ks], vn_ref[b, :, ks], cv_ref[b, :, ks]],
                                    axis=0).astype(BF16)
            q4 = jnp.concatenate([
                (_rope(q_ref[b, :, 128 * (4 * kh + g):128 * (4 * kh + g) + 128].astype(F32),
                       cosq_ref[...], sinq_ref[...]) * scale).astype(BF16) for g in range(4)], axis=0)
            s = jnp.where(mask, _dot_nt(q4, k_all), -jnp.inf)
            o = _sink_softmax_pv(s, _sink_column(sink_ref, kh, CHUNK), v_all)
            for g in range(4):
                hd = 4 * kh + g
                o_ref[b, :, 128 * hd:128 * hd + 128] = o[g * CHUNK:(g + 1) * CHUNK].astype(BF16)


def _lat_attn(q, k, v, cache_k, cache_v, cos2, sin2, sink_row):
    same = lambda qi: qi
    before = lambda qi: jnp.maximum(qi - 1, 0)
    after = lambda qi: jnp.minimum(qi + 1, N_QBLK - 1)
    blk_spec = lambda width, f: pl.BlockSpec((N_SAMPLE, CHUNK, width), lambda qi: (0, f(qi), 0))
    tab = lambda f: pl.BlockSpec((CHUNK, 128), lambda qi: (f(qi), 0))
    cache_spec = _const_spec((N_SAMPLE, N_CTX, 256))
    return pl.pallas_call(
        _lat_attn_kernel,
        grid=(N_QBLK,),
        in_specs=[blk_spec(D, same),
                  blk_spec(256, before), blk_spec(256, same), blk_spec(256, after),
                  blk_spec(256, before), blk_spec(256, same), blk_spec(256, after),
                  cache_spec, cache_spec,
                  tab(same), tab(same), tab(before), tab(before), tab(after), tab(after),
                  _const_spec((1, 128))],
        out_specs=blk_spec(D, same),
        out_shape=jax.ShapeDtypeStruct((N_SAMPLE, 2048, D), BF16),
        compiler_params=_params(("arbitrary",)),
        name="lat_attn",
    )(q, k, k, k, v, v, v, cache_k, cache_v, cos2, sin2, cos2, sin2, cos2, sin2, sink_row)


def _rope_tables():
    length = 2048
    rows = length // 64
    rowp = jnp.repeat(jnp.arange(rows, dtype=F32), 64)
    colp = jnp.tile(jnp.arange(64, dtype=F32), rows)
    inv = 10000.0 ** (-jnp.arange(32, dtype=F32) / 32)
    ang = jnp.concatenate([rowp[:, None] * inv, colp[:, None] * inv], axis=-1)
    cos, sin = jnp.cos(ang), jnp.sin(ang)
    return jnp.concatenate([cos, cos], axis=-1), jnp.concatenate([-sin, sin], axis=-1)


def kernel(x_prompt, x_sample, state_l0_ssd, state_l0_dn, cache_l1_k, cache_l1_v, c, c_ctx, mod_w_l0, mod_b_l0, norm_mix_pre_l0, norm_mix_post_l0, norm_ffn_pre_l0, norm_ffn_post_l0, ffn_up_l0, ffn_conv_w_l0, ffn_conv_b_l0, ffn_down_l0, mod_w_l1, mod_b_l1, norm_mix_pre_l1, norm_mix_post_l1, norm_ffn_pre_l1, norm_ffn_post_l1, ffn_up_l1, ffn_conv_w_l1, ffn_conv_b_l1, ffn_down_l1, mix_in_l0, mix_out_l0, ssd_conv_w, ssd_conv_b, ssd_dt_bias, ssd_A_log, ssd_D, ssd_norm_w, dn_conv_w, dn_dt_bias, dn_A_log, dn_norm_w, mix_in_l1, mix_out_l1, sg_ln_w, sg_ln_b, sg_w_s, sg_b_s, attn_sink):
    row = lambda a: a.reshape(1, -1).astype(F32)
    x_p = x_prompt.reshape(N_PROMPT * SEG, D)
    x_s = x_sample.reshape(N_SAMPLE * 2048, D)
    cond8 = jnp.concatenate([c_ctx[None, :], c, jnp.zeros((5, D), F32)], axis=0)
    mod0 = _adaln(cond8, mod_w_l0, mod_b_l0)
    mod1 = _adaln(cond8, mod_w_l1, mod_b_l1)

    w = mix_in_l0.astype(BF16)
    w_main = jnp.concatenate([w[:, 1024:3072], w[:, 3104:6176], w[:, 0:1024], w[:, 6176:7200],
                              w[:, 3072:3104], w[:, 7200:7232], jnp.zeros((D, 64), BF16)], axis=1)
    cw = jnp.concatenate([ssd_conv_w, dn_conv_w], axis=1)
    cb = jnp.concatenate([ssd_conv_b, jnp.zeros((3072,), F32)]).reshape(1, -1)
    xbc, qkv, z, gate, small = _l0_in(x_p, x_s, mod0, row(norm_mix_pre_l0), w_main, cw, cb)
    bias_col = jnp.concatenate([ssd_dt_bias.reshape(-1), dn_dt_bias.reshape(-1), jnp.zeros((80,), F32)]).reshape(128, 1)
    mult_col = jnp.concatenate([-jnp.exp(ssd_A_log.reshape(-1)), -jnp.exp(dn_A_log.reshape(-1)),
                                jnp.zeros((80,), F32)]).reshape(128, 1)
    rowf, colf = _prep(small, bias_col, mult_col)
    y, new_ssd = _ssd(xbc, rowf, colf, state_l0_ssd, jnp.repeat(ssd_D, 64).reshape(1, D))
    o, new_dn = _dn(qkv, rowf, colf, state_l0_dn)
    x = _l0_out(y, z, o, gate, x_p, x_s, mod0, mix_out_l0.astype(BF16), row(ssd_norm_w), row(dn_norm_w),
                row(norm_mix_post_l0))
    x = _ffn(x, mod0, row(norm_ffn_pre_l0), ffn_up_l0.astype(BF16), ffn_conv_w_l0, row(ffn_conv_b_l0),
             ffn_down_l0.astype(BF16), row(norm_ffn_post_l0))

    bs_full = jnp.repeat(sg_b_s.T, 256, axis=1)
    mlp, q_p, q_s, k_p, k_s, v_p, v_s = _l1_in(x, mod1, row(norm_mix_pre_l1), mix_in_l1.astype(BF16),
                                               row(sg_ln_w), row(sg_ln_b), sg_w_s.astype(BF16), bs_full)
    sink_row = jnp.pad(attn_sink, (0, 120)).reshape(1, 128)
    attn_p = _ctx_attn(q_p, k_p, v_p, sink_row)
    cos2, sin2 = _rope_tables()
    per_sample = lambda a: a.reshape(N_SAMPLE, 2048, a.shape[-1])
    attn_s = _lat_attn(per_sample(q_s), per_sample(k_s), per_sample(v_s),
                       cache_l1_k.reshape(N_SAMPLE, N_CTX, 256), cache_l1_v.reshape(N_SAMPLE, N_CTX, 256),
                       cos2, sin2, sink_row).reshape(N_SAMPLE * 2048, D)
    x = _l1_out(mlp, attn_p, attn_s, x, mod1, mix_out_l1.astype(BF16), row(norm_mix_post_l1))
    y_p, y_s = _ffn(x, mod1, row(norm_ffn_pre_l1), ffn_up_l1.astype(BF16), ffn_conv_w_l1, row(ffn_conv_b_l1),
                    ffn_down_l1.astype(BF16), row(norm_ffn_post_l1), split_out=True)

    return (y_p.reshape(N_PROMPT, SEG, D),
            y_s.reshape(N_SAMPLE, 2048, D),
            new_ssd,
            new_dn,
            k_p.reshape(N_PROMPT, SEG, 2, 128),
            v_p.reshape(N_PROMPT, SEG, 2, 128))
```

```python
import functools

import jax
import jax.numpy as jnp
from jax import lax
from jax.experimental import pallas as pl
from jax.experimental.pallas import tpu as pltpu

F32 = jnp.float32
BF16 = jnp.bfloat16

D = 1024
N_PROMPT = 32
SEG = 256
CHUNK = 128
SEG_PER_SAMPLE = 8
N_SAMPLE = 2
NSEG = N_PROMPT + N_SAMPLE * SEG_PER_SAMPLE
T = NSEG * SEG
HALO = 8
EPS = 1e-6
VMEM_LIMIT = 56 * 1024 * 1024

SSD_HEADS = 16
DN_HEADS = 8
D_FF = 2816
FF_CHUNK = 256

CH_DT = 0
CH_BETA = 48
CH_ACS = 64
CH_GCS = 96


def _dot(a, b):
    return jnp.dot(a, b, preferred_element_type=F32)


def _dot_nt(a, b):
    return lax.dot_general(a, b, (((1,), (1,)), ((), ())), preferred_element_type=F32)


def _dot_tn(a, b):
    return lax.dot_general(a, b, (((0,), (0,)), ((), ())), preferred_element_type=F32)


def _silu(x):
    return x * jax.nn.sigmoid(x)


def _rms(x, w):
    return x * lax.rsqrt(jnp.mean(x * x, axis=-1, keepdims=True) + EPS) * w


TOK = 512
NTOK_PROMPT = N_PROMPT * SEG // TOK
TOK_PER_SAMPLE = 2048 // TOK
NTOK = T // TOK


def _mod_index(i):
    return jnp.where(i < NTOK_PROMPT, 0, 1 + jnp.maximum(i - NTOK_PROMPT, 0) // TOK_PER_SAMPLE)


def _halo_flags(i):
    r = lax.rem(jnp.maximum(i - NTOK_PROMPT, 0), TOK_PER_SAMPLE)
    in_sample = i >= NTOK_PROMPT
    has_prev = jnp.where(jnp.logical_and(in_sample, r != 0), 1.0, 0.0)
    has_next = jnp.where(jnp.logical_and(in_sample, r != TOK_PER_SAMPLE - 1), 1.0, 0.0)
    return has_prev, has_next


def _row_spec(width):
    return pl.BlockSpec((TOK, width), lambda i: (i, 0))


def _prev_spec(first_tile=0, n_rows=T):
    return pl.BlockSpec((HALO, D), lambda i: (
        jnp.clip((i - first_tile) * (TOK // HALO) - 1, 0, n_rows // HALO - 1), 0))


def _next_spec(first_tile=0, n_rows=T):
    return pl.BlockSpec((HALO, D), lambda i: (
        jnp.clip((i - first_tile + 1) * (TOK // HALO), 0, n_rows // HALO - 1), 0))


def _prompt_spec(width):
    return pl.BlockSpec((TOK, width), lambda i: (jnp.minimum(i, NTOK_PROMPT - 1), 0))


def _sample_spec(width):
    return pl.BlockSpec((TOK, width), lambda i: (jnp.maximum(i - NTOK_PROMPT, 0), 0))


def _mod_spec():
    return pl.BlockSpec((1, 8, D), lambda i: (_mod_index(i), 0, 0))


def _const_spec(shape):
    nd = len(shape)
    return pl.BlockSpec(shape, lambda *_: (0,) * nd, pipeline_mode=pl.Buffered(1))


def _params(sem):
    return pltpu.CompilerParams(dimension_semantics=sem, vmem_limit_bytes=VMEM_LIMIT)


def _adaln_kernel(c_ref, w_ref, b_ref, o_ref):
    s = _silu(c_ref[...]).astype(BF16)
    o_ref[...] = _dot(s, w_ref[...].astype(BF16)) + b_ref[...]


def _adaln(cond8, w, b):
    tn = 768
    out = pl.pallas_call(
        _adaln_kernel,
        grid=(6 * D // tn,),
        in_specs=[pl.BlockSpec((8, D), lambda j: (0, 0)),
                  pl.BlockSpec((D, tn), lambda j: (0, j)),
                  pl.BlockSpec((1, tn), lambda j: (0, j))],
        out_specs=pl.BlockSpec((8, tn), lambda j: (0, j)),
        out_shape=jax.ShapeDtypeStruct((8, 6 * D), F32),
        compiler_params=_params(("arbitrary",)),
        name="adaln",
    )(cond8, w, b.reshape(1, -1))
    mod = out.reshape(8, 6, D)[:3]
    return jnp.pad(mod, ((0, 0), (0, 2), (0, 0)))


L0_CONV = 5120
L0_COLS = 7296
CONV_CHUNK = 512


def _pick_rows(i, prompt_ref, sample_ref):
    return jnp.where(i < NTOK_PROMPT, prompt_ref[...], sample_ref[...])


def _halo_rows(i, x, xp_ref, xn_ref):
    xe = jnp.concatenate([x, xn_ref[...], xp_ref[...]], axis=0)
    has_prev, has_next = _halo_flags(i)
    r = lax.broadcasted_iota(jnp.int32, (2 * HALO, 1), 0)
    return xe, (jnp.where(r < HALO, has_next, has_prev), jnp.where(i < NTOK_PROMPT, 1.0, 0.0))


def _dwconv(p, tile_info, w_ref, b_ref, cs, width):
    valid, two_seq = tile_info
    n = p.shape[0]
    p = jnp.concatenate([p[0:TOK], p[TOK:n] * valid], axis=0)
    half = width // 2
    taps = [p if k == half else pltpu.roll(p, (half - k) % n, 0) for k in range(width)]
    acc = b_ref[:, cs]
    for k in range(width):
        acc = acc + taps[k][0:TOK] * w_ref[k:k + 1, cs]
    lo, hi = SEG - 8, SEG + 8
    j = lax.broadcasted_iota(jnp.int32, (16, 1), 0)
    fix = b_ref[:, cs]
    for k in range(width):
        src = j + (k - half)
        crosses = jnp.where(j < 8, jnp.where(src >= 8, 1.0, 0.0), jnp.where(src < 8, 1.0, 0.0))
        keep = 1.0 - two_seq * crosses
        fix = fix + taps[k][lo:hi] * keep * w_ref[k:k + 1, cs]
    return jnp.concatenate([acc[0:lo], fix, acc[hi:TOK]], axis=0)


def _l0_in_kernel(zero_ref, xpr_ref, xsa_ref, xp_ref, xn_ref, m_ref, nw_ref, w_ref, cw_ref, cb_ref,
                  xbc_ref, qkv_ref, z_ref, gate_ref, small_ref, p_s):
    i = pl.program_id(0)
    shift = m_ref[0, 0:1, :]
    scale = m_ref[0, 1:2, :]
    xe, valid = _halo_rows(i, _pick_rows(i, xpr_ref, xsa_ref), xp_ref, xn_ref)
    he = (_rms(xe, nw_ref[...]) * (1.0 + scale) + shift).astype(BF16)
    hc = he[0:TOK]

    def plain(j):
        if j < 4:
            z_ref[:, 256 * j:256 * j + 256] = _dot(hc, w_ref[:, 5120 + 256 * j:5376 + 256 * j]).astype(BF16)
        elif j < 8:
            j -= 4
            gate_ref[:, 256 * j:256 * j + 256] = _dot(hc, w_ref[:, 6144 + 256 * j:6400 + 256 * j]).astype(BF16)
        elif j == 8:
            small_ref[...] = _dot(hc, w_ref[:, 7168:7296])

    z = zero_ref[0]
    n_conv = L0_CONV // CONV_CHUNK
    n_xbc = 2048 // CONV_CHUNK
    slots = CONV_CHUNK // 128
    plain_per_chunk = -(-9 // n_conv)
    conv_cols = lambda cc: slice(cc * CONV_CHUNK, (cc + 1) * CONV_CHUNK)
    p_s[z] = _dot(he, w_ref[:, conv_cols(0)])
    for cc in range(n_conv):
        cur = p_s[cc % 2 + z]
        if cc + 1 < n_conv:
            p_s[(cc + 1) % 2 + z] = _dot(he, w_ref[:, conv_cols(cc + 1)])
        for jj in range(cc * plain_per_chunk, (cc + 1) * plain_per_chunk):
            plain(jj)
        cs = conv_cols(cc)
        val = _silu(_dwconv(cur, valid, cw_ref, cb_ref, cs, 5)).astype(BF16)
        if cc < n_xbc:
            xbc_ref[:, cs] = val
        else:
            for t in range(slots):
                qkv_ref[(cc - n_xbc) * slots + t] = val[:, t * 128:(t + 1) * 128]


def _l0_in(x_prompt, x_sample, modt, nw, w_main, cw, cb):
    n_s = N_SAMPLE * 2048
    return pl.pallas_call(
        _l0_in_kernel,
        grid=(NTOK,),
        in_specs=[pl.BlockSpec(memory_space=pltpu.SMEM), _prompt_spec(D), _sample_spec(D),
                  _prev_spec(NTOK_PROMPT, n_s), _next_spec(NTOK_PROMPT, n_s), _mod_spec(),
                  _const_spec((1, D)), _const_spec((D, L0_COLS)),
                  _const_spec((5, L0_CONV)), _const_spec((1, L0_CONV))],
        out_specs=[_row_spec(2048),
                   pl.BlockSpec((24, TOK, 128), lambda i: (0, i, 0)),
                   _row_spec(D), _row_spec(D), _row_spec(128)],
        out_shape=[jax.ShapeDtypeStruct((T, 2048), BF16),
                   jax.ShapeDtypeStruct((24, T, 128), BF16),
                   jax.ShapeDtypeStruct((T, D), BF16),
                   jax.ShapeDtypeStruct((T, D), BF16),
                   jax.ShapeDtypeStruct((T, 128), F32)],
        scratch_shapes=[pltpu.VMEM((2, TOK + 2 * HALO, CONV_CHUNK), F32)],
        compiler_params=_params(("arbitrary",)),
        name="l0_in",
    )(jnp.zeros((1,), jnp.int32), x_prompt, x_sample, x_sample, x_sample, modt, nw, w_main, cw, cb)


def _split3_dot(a, tri):
    a1 = a.astype(BF16)
    r1 = a - a1.astype(F32)
    a2 = r1.astype(BF16)
    a3 = (r1 - a2.astype(F32)).astype(BF16)
    return _dot(a1, tri) + _dot(a2, tri) + _dot(a3, tri)


def _prep_kernel(s_ref, bias_ref, mult_ref, rowf_ref, colf_ref):
    row = lax.broadcasted_iota(jnp.int32, (CHUNK, CHUNK), 0)
    col = lax.broadcasted_iota(jnp.int32, (CHUNK, CHUNK), 1)
    upper = jnp.where(row <= col, 1.0, 0.0).astype(BF16)
    lower = jnp.where(row >= col, 1.0, 0.0).astype(BF16)
    backward = ((row >= 16) & (row < 32)) | ((row >= 40) & (row < 48))
    for c in range(TOK // CHUNK):
        rs = slice(c * CHUNK, (c + 1) * CHUNK)
        raw_t = s_ref[rs, :].T
        xb = raw_t + bias_ref[...]
        sp = jnp.maximum(xb, 0.0) + jnp.log1p(jnp.exp(-jnp.abs(xb)))
        vals = jnp.where(row < 48, sp, jax.nn.sigmoid(raw_t))
        a = jnp.where(row < 48, sp * mult_ref[...], 0.0)
        cum = jnp.where(backward, _split3_dot(a, lower), _split3_dot(a, upper))
        out = jnp.concatenate([vals[0:64], cum[0:64]], axis=0)
        rowf_ref[c] = out
        colf_ref[rs, :] = out.T


def _prep(small, bias_col, mult_col):
    return pl.pallas_call(
        _prep_kernel,
        grid=(NTOK,),
        in_specs=[_row_spec(128), _const_spec((128, 1)), _const_spec((128, 1))],
        out_specs=[pl.BlockSpec((TOK // CHUNK, CHUNK, CHUNK), lambda i: (i, 0, 0)), _row_spec(128)],
        out_shape=[jax.ShapeDtypeStruct((T // CHUNK, CHUNK, CHUNK), F32),
                   jax.ShapeDtypeStruct((T, 128), F32)],
        compiler_params=_params(("arbitrary",)),
        name="scan_prep",
    )(small, bias_col, mult_col)


def _scan_seg(d, j):
    return j + d * (NSEG - 1 - 2 * j)


def _scan_batch(seg):
    return jnp.maximum(seg - N_PROMPT, 0) // SEG_PER_SAMPLE


def _scan_specs(heads, rows):
    seg_map = lambda d, j: _scan_seg(d, j)
    return dict(
        rowf=pl.BlockSpec((2, CHUNK, CHUNK), lambda d, j: (seg_map(d, j), 0, 0)),
        colf=pl.BlockSpec((SEG, 128), lambda d, j: (seg_map(d, j), 0)),
        h0=pl.BlockSpec((1, 1, heads, rows, 128), lambda d, j: (_scan_batch(seg_map(d, j)), d, 0, 0, 0)),
        ns=pl.BlockSpec((1, 1, heads, rows, 128),
                        lambda d, j: (jnp.minimum(seg_map(d, j), N_PROMPT - 1), d, 0, 0, 0)),
    )


def _scan_init(seg, direction, st, h0_ref):
    is_prompt = seg < N_PROMPT
    r = lax.rem(jnp.maximum(seg - N_PROMPT, 0), SEG_PER_SAMPLE)
    first = 0 if direction == 0 else SEG_PER_SAMPLE - 1

    @pl.when(is_prompt)
    def _():
        st[...] = jnp.zeros_like(st)

    @pl.when(jnp.logical_and(jnp.logical_not(is_prompt), r == first))
    def _():
        st[...] = h0_ref[0, 0].reshape(st.shape)

    return is_prompt


def _ssd_body(direction, seg, xbc_ref, rowf_ref, colf_ref, h0_ref, drow_ref, y_ref, ns_ref,
              st, m_s, e_s, w_s, gl_s):
    is_prompt = _scan_init(seg, direction, st, h0_ref)
    row = lax.broadcasted_iota(jnp.int32, (CHUNK, CHUNK), 0)
    col = lax.broadcasted_iota(jnp.int32, (CHUNK, CHUNK), 1)
    mask = (col <= row) if direction == 0 else (col >= row)
    low = col < 64
    n_chunk = SEG // CHUNK
    order = range(n_chunk) if direction == 0 else range(n_chunk - 1, -1, -1)

    def decays():
        for c in range(n_chunk):
            rs = slice(c * CHUNK, (c + 1) * CHUNK)
            for g in range(4):
                bg = xbc_ref[rs, 1024 + 128 * g:1152 + 128 * g]
                cg = xbc_ref[rs, 1536 + 128 * g:1664 + 128 * g]
                gmat = _dot_nt(cg, bg)
                e_heads, w_heads = [], []
                for hh in range(4):
                    h = 4 * g + hh
                    ch_dt = CH_DT + 16 * direction + h
                    ch_ac = CH_ACS + 16 * direction + h
                    ar = rowf_ref[c, ch_ac:ch_ac + 1, :]
                    dt_r = rowf_ref[c, ch_dt:ch_dt + 1, :]
                    last = ar[:, CHUNK - 1:CHUNK] if direction == 0 else ar[:, 0:1]
                    ac_b = jnp.broadcast_to(colf_ref[rs, ch_ac:ch_ac + 1], (CHUNK, CHUNK))
                    dt_b = jnp.broadcast_to(colf_ref[rs, ch_dt:ch_dt + 1], (CHUNK, CHUNK))
                    lmat = jnp.exp(jnp.where(mask, ac_b - ar, -jnp.inf))
                    m_s[16 * c + h] = (gmat * lmat * dt_r).astype(BF16)
                    e_heads.append(jnp.exp(ac_b))
                    w_heads.append(dt_b * jnp.exp(last - ac_b))
                    gl_s[16 * c + h] = jnp.broadcast_to(jnp.exp(last), (8, CHUNK))
                pair = lambda t: jnp.concatenate([jnp.where(low, t[0], t[1]), jnp.where(low, t[2], t[3])],
                                                 axis=1)
                e_s[4 * c + g] = pair(e_heads)
                w_s[4 * c + g] = pair(w_heads)

    def outputs():
        for c in order:
            rs = slice(c * CHUNK, (c + 1) * CHUNK)
            for g in range(4):
                bg = xbc_ref[rs, 1024 + 128 * g:1152 + 128 * g]
                cg = xbc_ref[rs, 1536 + 128 * g:1664 + 128 * g]
                xf = xbc_ref[rs, 256 * g:256 * g + 256].astype(F32)
                y_in = _dot_nt(cg, st[256 * g:256 * g + 256, :].astype(BF16)) * e_s[4 * c + g]
                y_halves = []
                for half in range(2):
                    xh = xf[:, 128 * half:128 * half + 128]
                    m0 = m_s[16 * c + 4 * g + 2 * half]
                    m1 = m_s[16 * c + 4 * g + 2 * half + 1]
                    y_halves.append(y_in[:, 128 * half:128 * half + 128]
                                    + _dot(m0, jnp.where(low, xh, 0.0).astype(BF16))
                                    + _dot(m1, jnp.where(low, 0.0, xh).astype(BF16)))
                y = jnp.concatenate(y_halves, axis=1)
                if direction == 0:
                    y = y + xf * drow_ref[:, 256 * g:256 * g + 256]
                y_ref[0, rs, 256 * g:256 * g + 256] = y.astype(BF16)
                upd = _dot_tn((xf * w_s[4 * c + g]).astype(BF16), bg)
                for hh in range(4):
                    hr = slice(256 * g + 64 * hh, 256 * g + 64 * hh + 64)
                    st[hr, :] = st[hr, :] * gl_s[16 * c + 4 * g + hh, 0:1, :] + upd[64 * hh:64 * hh + 64]

    decays()
    outputs()

    @pl.when(is_prompt)
    def _():
        ns_ref[0, 0] = st[...].reshape(SSD_HEADS, 64, 128)


def _ssd_kernel(xbc_ref, rowf_ref, colf_ref, h0_ref, drow_ref, y_ref, ns_ref, *scratch):
    d = pl.program_id(0)
    seg = _scan_seg(d, pl.program_id(1))
    for direction in (0, 1):
        pl.when(d == direction)(functools.partial(
            _ssd_body, direction, seg, xbc_ref, rowf_ref, colf_ref, h0_ref, drow_ref, y_ref, ns_ref, *scratch))


def _ssd(xbc, rowf, colf, h0, drow):
    sp = _scan_specs(SSD_HEADS, 64)
    n_chunk = SEG // CHUNK
    return pl.pallas_call(
        _ssd_kernel,
        grid=(2, NSEG),
        in_specs=[pl.BlockSpec((SEG, 2048), lambda d, j: (_scan_seg(d, j), 0)),
                  sp["rowf"], sp["colf"], sp["h0"],
                  pl.BlockSpec((1, D), lambda d, j: (0, 0))],
        out_specs=[pl.BlockSpec((1, SEG, D), lambda d, j: (d, _scan_seg(d, j), 0)), sp["ns"]],
        out_shape=[jax.ShapeDtypeStruct((2, T, D), BF16),
                   jax.ShapeDtypeStruct((N_PROMPT, 2, SSD_HEADS, 64, 128), F32)],
        scratch_shapes=[pltpu.VMEM((1024, 128), F32),
                        pltpu.VMEM((n_chunk * SSD_HEADS, CHUNK, CHUNK), BF16),
                        pltpu.VMEM((n_chunk * 4, CHUNK, 256), F32),
                        pltpu.VMEM((n_chunk * 4, CHUNK, 256), F32),
                        pltpu.VMEM((n_chunk * SSD_HEADS, 8, CHUNK), F32)],
        compiler_params=_params(("arbitrary", "arbitrary")),
        name="ssd_scan",
    )(xbc, rowf, colf, h0, drow)


def _l2n(x):
    return x * lax.rsqrt(jnp.sum(x * x, axis=-1, keepdims=True) + EPS)


def _dn_body(direction, parity, j, seg_b, qkv_ref, rowf_ref, colf_ref, s0_ref, lvl_ref, o_ref, ns_ref,
             st, a_s, d_s, rhs_s, qk_s, qd_s, ke_s, u_s, w_s, gl_s):
    cur, prev = parity, 1 - parity
    is_prompt = _scan_init(seg_b, direction, st, s0_ref)
    row = lax.broadcasted_iota(jnp.int32, (CHUNK, CHUNK), 0)
    col = lax.broadcasted_iota(jnp.int32, (CHUNK, CHUNK), 1)
    tri = (col <= row) if direction == 0 else (col >= row)
    strict = (col < row) if direction == 0 else (col > row)
    eye = jnp.where(row == col, 1.0, 0.0)
    n_chunk = SEG // CHUNK
    order = range(n_chunk) if direction == 0 else range(n_chunk - 1, -1, -1)

    def gram(h, c):
        rs = slice(c * CHUNK, (c + 1) * CHUNK)
        ch_b = CH_BETA + 8 * direction + h
        qn = _l2n(qkv_ref[h, rs, :].astype(F32)) * (128 ** -0.5)
        kn = _l2n(qkv_ref[DN_HEADS + h, rs, :].astype(F32))
        kb = kn * colf_ref[rs, ch_b:ch_b + 1]
        return qn, kn, kb, _dot_nt(jnp.concatenate([kb, qn], axis=0).astype(BF16), kn.astype(BF16))

    def operands(h, c, qn, kn, kb, kq):
        ch_b = CH_BETA + 8 * direction + h
        ch_g = CH_GCS + 8 * direction + h
        p = n_chunk * h + c
        rs = slice(c * CHUNK, (c + 1) * CHUNK)
        v = qkv_ref[2 * DN_HEADS + h, rs, :].astype(F32)
        beta = colf_ref[rs, ch_b:ch_b + 1]
        gc = colf_ref[rs, ch_g:ch_g + 1]
        gr = rowf_ref[c, ch_g:ch_g + 1, :]
        glast = gr[:, CHUNK - 1:CHUNK] if direction == 0 else gr[:, 0:1]
        decay = jnp.exp(jnp.where(tri, gc - gr, -jnp.inf))
        a = jnp.where(strict, kq[0:CHUNK] * decay, 0.0)
        a_s[cur, p] = a.astype(BF16)
        d_s[cur, p] = eye - a * lvl_ref[0].astype(F32)
        qk_s[cur, p] = (kq[CHUNK:2 * CHUNK] * decay).astype(BF16)
        eg = jnp.exp(gc)
        rhs_s[cur, p] = jnp.concatenate([v * beta, kb * eg], axis=1).astype(BF16)
        qd_s[cur, p] = (qn * eg).astype(BF16)
        ke_s[cur, p] = (kn * jnp.exp(glast - gc)).astype(BF16)
        gl_s[cur, p] = jnp.broadcast_to(jnp.exp(glast), (8, CHUNK))

    def double(lvl, p):
        tm = d_s[prev, p]
        tb = tm.astype(BF16)
        a_off = a_s[prev, p] * lvl_ref[lvl]
        d_s[prev, p] = tm - _dot(tb, _dot(a_off, tb).astype(BF16))

    for lvl in range(1, 7):
        for p in range(n_chunk * DN_HEADS):
            double(lvl, p)
    for h in range(DN_HEADS):
        for c in range(n_chunk):
            operands(h, c, *gram(h, c))

    for p in range(n_chunk * DN_HEADS):
        uw = _dot(d_s[prev, p].astype(BF16), rhs_s[prev, p])
        u_s[p] = uw[:, 0:CHUNK]
        w_s[p] = uw[:, CHUNK:2 * CHUNK].astype(BF16)

    for c in order:
        wq = [_dot(jnp.concatenate([w_s[n_chunk * h + c], qd_s[prev, n_chunk * h + c]], axis=0),
                   st[h].astype(BF16)) for h in range(DN_HEADS)]
        for h in range(DN_HEADS):
            p = n_chunk * h + c
            vb = (u_s[p] - wq[h][0:CHUNK]).astype(BF16)
            o = wq[h][CHUNK:2 * CHUNK] + _dot(qk_s[prev, p], vb)
            st[h] = st[h] * gl_s[prev, p, 0:1, :] + _dot_tn(ke_s[prev, p], vb)
            o_ref[0, h, c * CHUNK:(c + 1) * CHUNK, :] = o.astype(BF16)

    @pl.when(jnp.logical_and(is_prompt, j >= 1))
    def _():
        ns_ref[0, 0] = st[...]


def _dn_seg_in(d, j):
    return _scan_seg(d, jnp.minimum(j, NSEG - 1))


def _dn_seg_out(d, j):
    return _scan_seg(d, jnp.maximum(j - 1, 0))


def _dn_kernel(qkv_ref, rowf_ref, colf_ref, s0_ref, lvl_ref, o_ref, ns_ref, *scratch):
    d = pl.program_id(0)
    j = pl.program_id(1)

    @pl.when(jnp.logical_and(d == 0, j == 0))
    def _():
        for ref in scratch:
            ref[...] = jnp.zeros_like(ref)

    for direction in (0, 1):
        for parity in (0, 1):
            pl.when(jnp.logical_and(d == direction, lax.rem(j, 2) == parity))(functools.partial(
                _dn_body, direction, parity, j, _dn_seg_out(d, j), qkv_ref, rowf_ref, colf_ref, s0_ref,
                lvl_ref, o_ref, ns_ref, *scratch))


def _sibling_masks():
    i = jnp.arange(CHUNK)[:, None]
    j = jnp.arange(CHUNK)[None, :]
    lv = [((i >> (l + 1)) == (j >> (l + 1))) & ((i >> l) != (j >> l)) for l in range(7)]
    return jnp.stack(lv).astype(BF16)


def _dn(qkv, rowf, colf, s0):
    n_prob = DN_HEADS * SEG // CHUNK
    state = (1, 1, DN_HEADS, CHUNK, CHUNK)
    two = lambda dt, *shape: pltpu.VMEM((2, n_prob) + shape, dt)
    return pl.pallas_call(
        _dn_kernel,
        grid=(2, NSEG + 1),
        in_specs=[pl.BlockSpec((24, SEG, 128), lambda d, j: (0, _dn_seg_in(d, j), 0)),
                  pl.BlockSpec((2, CHUNK, CHUNK), lambda d, j: (_dn_seg_in(d, j), 0, 0)),
                  pl.BlockSpec((SEG, 128), lambda d, j: (_dn_seg_in(d, j), 0)),
                  pl.BlockSpec(state, lambda d, j: (_scan_batch(_dn_seg_out(d, j)), d, 0, 0, 0)),
                  _const_spec((7, CHUNK, CHUNK))],
        out_specs=[pl.BlockSpec((1, DN_HEADS, SEG, 128), lambda d, j: (d, 0, _dn_seg_out(d, j), 0)),
                   pl.BlockSpec(state, lambda d, j: (jnp.minimum(_dn_seg_out(d, j), N_PROMPT - 1), d, 0, 0, 0))],
        out_shape=[jax.ShapeDtypeStruct((2, DN_HEADS, T, 128), BF16),
                   jax.ShapeDtypeStruct((N_PROMPT, 2, DN_HEADS, CHUNK, CHUNK), F32)],
        scratch_shapes=[pltpu.VMEM((DN_HEADS, CHUNK, CHUNK), F32),
                        two(BF16, CHUNK, CHUNK), two(F32, CHUNK, CHUNK),
                        two(BF16, CHUNK, 2 * CHUNK),
                        two(BF16, CHUNK, CHUNK), two(BF16, CHUNK, CHUNK),
                        two(BF16, CHUNK, CHUNK),
                        pltpu.VMEM((n_prob, CHUNK, CHUNK), F32),
                        pltpu.VMEM((n_prob, CHUNK, CHUNK), BF16),
                        two(F32, 8, CHUNK)],
        compiler_params=_params(("arbitrary", "arbitrary")),
        name="dn_scan",
    )(qkv, rowf, colf, s0, _sibling_masks())


def _residual(x, m_ref, gate_row, out, pw_ref):
    gate = m_ref[0, gate_row:gate_row + 1, :]
    return x + gate * _rms(out, pw_ref[...])


def _l0_out_kernel(y_ref, z_ref, o_ref, gate_ref, xpr_ref, xsa_ref, m_ref, w_ref, snw_ref, dnw_ref, pw_ref,
                   xo_ref):
    i = pl.program_id(0)

    def project(rs):
        y = (y_ref[0, rs, :].astype(F32) + y_ref[1, rs, :].astype(F32)) * _silu(z_ref[rs, :].astype(F32))
        yn = _rms(y, snw_ref[...]).astype(BF16)
        heads = []
        for h in range(DN_HEADS):
            oh = o_ref[0, h, rs, :].astype(F32) + o_ref[1, h, rs, :].astype(F32)
            gh = gate_ref[rs, 128 * h:128 * h + 128].astype(F32)
            heads.append((_rms(oh, dnw_ref[...]) * _silu(gh)).astype(BF16))
        on = jnp.concatenate(heads, axis=1)
        return _dot(yn, w_ref[0:1024, :]) + _dot(on, w_ref[1024:2048, :])

    halves = [slice(0, TOK // 2), slice(TOK // 2, TOK)]
    outs = [project(rs) for rs in halves]
    for rs, out in zip(halves, outs):
        x = jnp.where(i < NTOK_PROMPT, xpr_ref[rs, :], xsa_ref[rs, :])
        xo_ref[rs, :] = _residual(x, m_ref, 2, out, pw_ref)


def _l0_out(y, z, o, gate, x_prompt, x_sample, modt, w_out, snw, dnw, pw):
    return pl.pallas_call(
        _l0_out_kernel,
        grid=(NTOK,),
        in_specs=[pl.BlockSpec((2, TOK, D), lambda i: (0, i, 0)), _row_spec(D),
                  pl.BlockSpec((2, DN_HEADS, TOK, 128), lambda i: (0, 0, i, 0)), _row_spec(D),
                  _prompt_spec(D), _sample_spec(D), _mod_spec(), _const_spec((2048, D)),
                  _const_spec((1, D)), _const_spec((1, 128)), _const_spec((1, D))],
        out_specs=_row_spec(D),
        out_shape=jax.ShapeDtypeStruct((T, D), F32),
        compiler_params=_params(("arbitrary",)),
        name="l0_out",
    )(y, z, o, gate, x_prompt, x_sample, modt, w_out, snw, dnw, pw)


def _l1_out_kernel(a_ref, bp_ref, bs_ref, x_ref, m_ref, w_ref, pw_ref, xo_ref):
    def finish(b_ref):
        out = _dot(a_ref[...], w_ref[0:1024, :]) + _dot(b_ref[...], w_ref[1024:2048, :])
        xo_ref[...] = _residual(x_ref[...], m_ref, 2, out, pw_ref)

    is_prompt = pl.program_id(0) < NTOK_PROMPT
    pl.when(is_prompt)(functools.partial(finish, bp_ref))
    pl.when(jnp.logical_not(is_prompt))(functools.partial(finish, bs_ref))


def _l1_out(a, b_prompt, b_sample, x, modt, w_out, pw):
    return pl.pallas_call(
        _l1_out_kernel,
        grid=(NTOK,),
        in_specs=[_row_spec(D), _prompt_spec(D), _sample_spec(D),
                  _row_spec(D), _mod_spec(), _const_spec((2048, D)), _const_spec((1, D))],
        out_specs=_row_spec(D),
        out_shape=jax.ShapeDtypeStruct((T, D), F32),
        compiler_params=_params(("arbitrary",)),
        name="l1_out",
    )(a, b_prompt, b_sample, x, modt, w_out, pw)


def _store_split(i, val, prompt_ref, sample_ref):
    @pl.when(i < NTOK_PROMPT)
    def _():
        prompt_ref[...] = val

    @pl.when(i >= NTOK_PROMPT)
    def _():
        sample_ref[...] = val


def _ffn_kernel(zero_ref, x_ref, xp_ref, xn_ref, m_ref, nw_ref, wu_ref, cw_ref, cb_ref, wd_ref, pw_ref,
                *out_and_scratch):
    *out_refs, p_s = out_and_scratch
    i = pl.program_id(0)
    z = zero_ref[0]
    shift = m_ref[0, 3:4, :]
    scale = m_ref[0, 4:5, :]
    xe, valid = _halo_rows(i, x_ref[...], xp_ref, xn_ref)
    he = (_rms(xe, nw_ref[...]) * (1.0 + scale) + shift).astype(BF16)
    n_chunk = D_FF // FF_CHUNK
    cols = lambda cc: [slice(base + cc * FF_CHUNK, base + (cc + 1) * FF_CHUNK) for base in (0, D_FF)]

    def up(cc):
        for half, cs in enumerate(cols(cc)):
            p_s[cc % 2 + z, half] = _dot(he, wu_ref[:, cs])

    acc = jnp.zeros((TOK, D), F32)
    up(0)
    for cc in range(n_chunk):
        cur = [p_s[cc % 2 + z, half] for half in range(2)]
        if cc + 1 < n_chunk:
            up(cc + 1)
        a, b = [_dwconv(p, valid, cw_ref, cb_ref, cs, 3) for p, cs in zip(cur, cols(cc))]
        g = (_silu(a) * b).astype(BF16)
        acc = acc + _dot(g, wd_ref[cc * FF_CHUNK:(cc + 1) * FF_CHUNK, :])
    xo = _residual(x_ref[...], m_ref, 5, acc, pw_ref)
    if len(out_refs) == 1:
        out_refs[0][...] = xo
    else:
        _store_split(i, xo, *out_refs)


def _ffn(x, modt, nw, wu, cw, cb, wd, pw, split_out=False):
    if split_out:
        out_specs = [_prompt_spec(D), _sample_spec(D)]
        out_shape = [jax.ShapeDtypeStruct((N_PROMPT * SEG, D), F32),
                     jax.ShapeDtypeStruct((N_SAMPLE * 2048, D), F32)]
    else:
        out_specs = _row_spec(D)
        out_shape = jax.ShapeDtypeStruct((T, D), F32)
    return pl.pallas_call(
        _ffn_kernel,
        grid=(NTOK,),
        in_specs=[pl.BlockSpec(memory_space=pltpu.SMEM),
                  _row_spec(D), _prev_spec(), _next_spec(), _mod_spec(), _const_spec((1, D)),
                  _const_spec((D, 2 * D_FF)), _const_spec((3, 2 * D_FF)), _const_spec((1, 2 * D_FF)),
                  _const_spec((D_FF, D)), _const_spec((1, D))],
        out_specs=out_specs,
        out_shape=out_shape,
        scratch_shapes=[pltpu.VMEM((2, 2, TOK + 2 * HALO, FF_CHUNK), F32)],
        compiler_params=_params(("arbitrary",)),
        name="conv_ffn",
    )(jnp.zeros((1,), jnp.int32), x, x, x, modt, nw, wu, cw, cb, wd, pw)


def _l1_in_kernel(x_ref, m_ref, nw_ref, w_ref, lnw_ref, lnb_ref, ws_ref, bs_ref,
                  mlp_ref, qp_ref, qs_ref, kp_ref, ks_ref, vp_ref, vs_ref):
    i = pl.program_id(0)
    shift = m_ref[0, 0:1, :]
    scale = m_ref[0, 1:2, :]
    h = (_rms(x_ref[...], nw_ref[...]) * (1.0 + scale) + shift).astype(BF16)
    gv = _dot(h, w_ref[:, 1024:2048])
    q = _dot(h, w_ref[:, 2048:3072]).astype(BF16)
    k = _dot(h, w_ref[:, 3072:3328])
    v = _dot(h, w_ref[:, 3328:3584])
    u = _dot(h, w_ref[:, 0:1024])
    gv = jax.nn.gelu(gv)
    gc = gv - jnp.mean(gv, axis=-1, keepdims=True)
    gn = gc * lax.rsqrt(jnp.mean(gc * gc, axis=-1, keepdims=True) + EPS) * lnw_ref[...] + lnb_ref[...]
    gb = gn.astype(BF16)
    u = jax.nn.gelu(u)
    for c in range(TOK // CHUNK):
        rs = slice(c * CHUNK, (c + 1) * CHUNK)
        for g in range(4):
            cs = slice(256 * g, 256 * g + 256)
            sv = _dot(ws_ref[g], gb[rs, cs]) + bs_ref[:, cs]
            mlp_ref[rs, cs] = (u[rs, cs] * sv).astype(BF16)
    _store_split(i, q, qp_ref, qs_ref)
    _store_split(i, k, kp_ref, ks_ref)
    _store_split(i, v, vp_ref, vs_ref)


def _l1_in(x, modt, nw, w_in, lnw, lnb, ws, bs):
    n_p, n_s = N_PROMPT * SEG, N_SAMPLE * 2048
    return pl.pallas_call(
        _l1_in_kernel,
        grid=(NTOK,),
        in_specs=[_row_spec(D), _mod_spec(), _const_spec((1, D)), _const_spec((D, 3584)),
                  _const_spec((1, D)), _const_spec((1, D)), _const_spec((4, CHUNK, CHUNK)),
                  _const_spec((CHUNK, D))],
        out_specs=[_row_spec(D), _prompt_spec(D), _sample_spec(D), _prompt_spec(256), _sample_spec(256),
                   _prompt_spec(256), _sample_spec(256)],
        out_shape=[jax.ShapeDtypeStruct((T, D), BF16),
                   jax.ShapeDtypeStruct((n_p, D), BF16), jax.ShapeDtypeStruct((n_s, D), BF16),
                   jax.ShapeDtypeStruct((n_p, 256), F32), jax.ShapeDtypeStruct((n_s, 256), F32),
                   jax.ShapeDtypeStruct((n_p, 256), F32), jax.ShapeDtypeStruct((n_s, 256), F32)],
        compiler_params=_params(("arbitrary",)),
        name="l1_in",
    )(x, modt, nw, w_in, lnw, lnb, ws, bs)


def _sink_softmax_pv(s, sink_col, vb):
    m = jnp.maximum(jnp.max(s, axis=-1, keepdims=True), sink_col)
    p = jnp.exp(s - m)
    den = jnp.sum(p, axis=-1, keepdims=True) + jnp.exp(sink_col - m)
    return _dot(p.astype(BF16), vb) / den


def _sink_column(sink_ref, kh, rows):
    parts = [jnp.broadcast_to(sink_ref[:, 4 * kh + g:4 * kh + g + 1], (rows, 1)) for g in range(4)]
    return jnp.concatenate(parts, axis=0)


CTX_SEQS = 8


def _ctx_attn_kernel(q_ref, k_ref, v_ref, sink_ref, o_ref):
    scale = 128 ** -0.5
    for b in range(CTX_SEQS):
        rows = slice(b * SEG, (b + 1) * SEG)
        for kh in range(2):
            kb = k_ref[rows, 128 * kh:128 * kh + 128].astype(BF16)
            vb = v_ref[rows, 128 * kh:128 * kh + 128].astype(BF16)
            q4 = jnp.concatenate([q_ref[rows, 128 * (4 * kh + g):128 * (4 * kh + g) + 128] for g in range(4)],
                                 axis=0)
            s = _dot_nt(q4, kb) * scale
            o = _sink_softmax_pv(s, _sink_column(sink_ref, kh, SEG), vb)
            for g in range(4):
                hd = 4 * kh + g
                o_ref[rows, 128 * hd:128 * hd + 128] = o[g * SEG:(g + 1) * SEG].astype(BF16)


def _ctx_attn(q, k, v, sink_row):
    seq_spec = lambda width: pl.BlockSpec((CTX_SEQS * SEG, width), lambda b: (b, 0))
    return pl.pallas_call(
        _ctx_attn_kernel,
        grid=(N_PROMPT // CTX_SEQS,),
        in_specs=[seq_spec(D), seq_spec(256), seq_spec(256), _const_spec((1, 128))],
        out_specs=seq_spec(D),
        out_shape=jax.ShapeDtypeStruct((N_PROMPT * SEG, D), BF16),
        compiler_params=_params(("arbitrary",)),
        name="ctx_attn",
    )(q, k, v, sink_row)


N_QBLK = 2048 // CHUNK
N_CTX = 512


def _rope(x, cos2, sin2):
    return x * cos2 + pltpu.roll(x, 64, 1) * sin2


def _lat_attn_kernel(q_ref, kp_ref, kc_ref, kn_ref, vp_ref, vc_ref, vn_ref, ck_ref, cv_ref,
                     cosq_ref, sinq_ref, cosp_ref, sinp_ref, cosn_ref, sinn_ref, sink_ref, o_ref):
    qi = pl.program_id(0)
    scale = 128 ** -0.5
    nk = 3 * CHUNK + N_CTX
    r = lax.broadcasted_iota(jnp.int32, (4 * CHUNK, nk), 0) & (CHUNK - 1)
    c = lax.broadcasted_iota(jnp.int32, (4 * CHUNK, nk), 1)
    far = 4 * CHUNK
    is_prev = c < CHUNK
    is_next = (c >= 2 * CHUNK) & (c < 3 * CHUNK)
    ok_prev = is_prev & (c >= r + jnp.where(qi > 0, 0, far))
    ok_next = is_next & (c - 2 * CHUNK <= r - jnp.where(qi < N_QBLK - 1, 0, far))
    mask = ok_prev | ok_next | jnp.logical_not(is_prev | is_next)
    for b in range(N_SAMPLE):
        for kh in range(2):
            ks = slice(128 * kh, 128 * kh + 128)
            k_all = jnp.concatenate([
                _rope(kp_ref[b, :, ks], cosp_ref[...], sinp_ref[...]).astype(BF16),
                _rope(kc_ref[b, :, ks], cosq_ref[...], sinq_ref[...]).astype(BF16),
                _rope(kn_ref[b, :, ks], cosn_ref[...], sinn_ref[...]).astype(BF16),
                ck_ref[b, :, ks].astype(BF16)], axis=0)
            v_all = jnp.concatenate([vp_ref[b, :, ks], vc_ref[b, :, ks], vn_ref[b, :, ks], cv_ref[b, :, ks]],
                                    axis=0).astype(BF16)
            q4 = jnp.concatenate([
                (_rope(q_ref[b, :, 128 * (4 * kh + g):128 * (4 * kh + g) + 128].astype(F32),
                       cosq_ref[...], sinq_ref[...]) * scale).astype(BF16) for g in range(4)], axis=0)
            s = jnp.where(mask, _dot_nt(q4, k_all), -jnp.inf)
            o = _sink_softmax_pv(s, _sink_column(sink_ref, kh, CHUNK), v_all)
            for g in range(4):
                hd = 4 * kh + g
                o_ref[b, :, 128 * hd:128 * hd + 128] = o[g * CHUNK:(g + 1) * CHUNK].astype(BF16)


def _lat_attn(q, k, v, cache_k, cache_v, cos2, sin2, sink_row):
    same = lambda qi: qi
    before = lambda qi: jnp.maximum(qi - 1, 0)
    after = lambda qi: jnp.minimum(qi + 1, N_QBLK - 1)
    blk_spec = lambda width, f: pl.BlockSpec((N_SAMPLE, CHUNK, width), lambda qi: (0, f(qi), 0))
    tab = lambda f: pl.BlockSpec((CHUNK, 128), lambda qi: (f(qi), 0))
    cache_spec = _const_spec((N_SAMPLE, N_CTX, 256))
    return pl.pallas_call(
        _lat_attn_kernel,
        grid=(N_QBLK,),
        in_specs=[blk_spec(D, same),
                  blk_spec(256, before), blk_spec(256, same), blk_spec(256, after),
                  blk_spec(256, before), blk_spec(256, same), blk_spec(256, after),
                  cache_spec, cache_spec,
                  tab(same), tab(same), tab(before), tab(before), tab(after), tab(after),
                  _const_spec((1, 128))],
        out_specs=blk_spec(D, same),
        out_shape=jax.ShapeDtypeStruct((N_SAMPLE, 2048, D), BF16),
        compiler_params=_params(("arbitrary",)),
        name="lat_attn",
    )(q, k, k, k, v, v, v, cache_k, cache_v, cos2, sin2, cos2, sin2, cos2, sin2, sink_row)


def _rope_tables():
    length = 2048
    rows = length // 64
    rowp = jnp.repeat(jnp.arange(rows, dtype=F32), 64)
    colp = jnp.tile(jnp.arange(64, dtype=F32), rows)
    inv = 10000.0 ** (-jnp.arange(32, dtype=F32) / 32)
    ang = jnp.concatenate([rowp[:, None] * inv, colp[:, None] * inv], axis=-1)
    cos, sin = jnp.cos(ang), jnp.sin(ang)
    return jnp.concatenate([cos, cos], axis=-1), jnp.concatenate([-sin, sin], axis=-1)


def kernel(x_prompt, x_sample, state_l0_ssd, state_l0_dn, cache_l1_k, cache_l1_v, c, c_ctx, mod_w_l0, mod_b_l0, norm_mix_pre_l0, norm_mix_post_l0, norm_ffn_pre_l0, norm_ffn_post_l0, ffn_up_l0, ffn_conv_w_l0, ffn_conv_b_l0, ffn_down_l0, mod_w_l1, mod_b_l1, norm_mix_pre_l1, norm_mix_post_l1, norm_ffn_pre_l1, norm_ffn_post_l1, ffn_up_l1, ffn_conv_w_l1, ffn_conv_b_l1, ffn_down_l1, mix_in_l0, mix_out_l0, ssd_conv_w, ssd_conv_b, ssd_dt_bias, ssd_A_log, ssd_D, ssd_norm_w, dn_conv_w, dn_dt_bias, dn_A_log, dn_norm_w, mix_in_l1, mix_out_l1, sg_ln_w, sg_ln_b, sg_w_s, sg_b_s, attn_sink):
    row = lambda a: a.reshape(1, -1).astype(F32)
    x_p = x_prompt.reshape(N_PROMPT * SEG, D)
    x_s = x_sample.reshape(N_SAMPLE * 2048, D)
    cond8 = jnp.concatenate([c_ctx[None, :], c, jnp.zeros((5, D), F32)], axis=0)
    mod0 = _adaln(cond8, mod_w_l0, mod_b_l0)
    mod1 = _adaln(cond8, mod_w_l1, mod_b_l1)

    w = mix_in_l0.astype(BF16)
    w_main = jnp.concatenate([w[:, 1024:3072], w[:, 3104:6176], w[:, 0:1024], w[:, 6176:7200],
                              w[:, 3072:3104], w[:, 7200:7232], jnp.zeros((D, 64), BF16)], axis=1)
    cw = jnp.concatenate([ssd_conv_w, dn_conv_w], axis=1)
    cb = jnp.concatenate([ssd_conv_b, jnp.zeros((3072,), F32)]).reshape(1, -1)
    xbc, qkv, z, gate, small = _l0_in(x_p, x_s, mod0, row(norm_mix_pre_l0), w_main, cw, cb)
    bias_col = jnp.concatenate([ssd_dt_bias.reshape(-1), dn_dt_bias.reshape(-1), jnp.zeros((80,), F32)]).reshape(128, 1)
    mult_col = jnp.concatenate([-jnp.exp(ssd_A_log.reshape(-1)), -jnp.exp(dn_A_log.reshape(-1)),
                                jnp.zeros((80,), F32)]).reshape(128, 1)
    rowf, colf = _prep(small, bias_col, mult_col)
    y, new_ssd = _ssd(xbc, rowf, colf, state_l0_ssd, jnp.repeat(ssd_D, 64).reshape(1, D))
    o, new_dn = _dn(qkv, rowf, colf, state_l0_dn)
    x = _l0_out(y, z, o, gate, x_p, x_s, mod0, mix_out_l0.astype(BF16), row(ssd_norm_w), row(dn_norm_w),
                row(norm_mix_post_l0))
    x = _ffn(x, mod0, row(norm_ffn_pre_l0), ffn_up_l0.astype(BF16), ffn_conv_w_l0, row(ffn_conv_b_l0),
             ffn_down_l0.astype(BF16), row(norm_ffn_post_l0))

    bs_full = jnp.repeat(sg_b_s.T, 256, axis=1)
    mlp, q_p, q_s, k_p, k_s, v_p, v_s = _l1_in(x, mod1, row(norm_mix_pre_l1), mix_in_l1.astype(BF16),
                                               row(sg_ln_w), row(sg_ln_b), sg_w_s.astype(BF16), bs_full)
    sink_row = jnp.pad(attn_sink, (0, 120)).reshape(1, 128)
    attn_p = _ctx_attn(q_p, k_p, v_p, sink_row)
    cos2, sin2 = _rope_tables()
    per_sample = lambda a: a.reshape(N_SAMPLE, 2048, a.shape[-1])
    attn_s = _lat_attn(per_sample(q_s), per_sample(k_s), per_sample(v_s),
                       cache_l1_k.reshape(N_SAMPLE, N_CTX, 256), cache_l1_v.reshape(N_SAMPLE, N_CTX, 256),
                       cos2, sin2, sink_row).reshape(N_SAMPLE * 2048, D)
    x = _l1_out(mlp, attn_p, attn_s, x, mod1, mix_out_l1.astype(BF16), row(norm_mix_post_l1))
    y_p, y_s = _ffn(x, mod1, row(norm_ffn_pre_l1), ffn_up_l1.astype(BF16), ffn_conv_w_l1, row(ffn_conv_b_l1),
                    ffn_down_l1.astype(BF16), row(norm_ffn_post_l1), split_out=True)

    return (y_p.reshape(N_PROMPT, SEG, D),
            y_s.reshape(N_SAMPLE, 2048, D),
            new_ssd,
            new_dn,
            k_p.reshape(N_PROMPT, SEG, 2, 128),
            v_p.reshape(N_PROMPT, SEG, 2, 128))
```
